```python
import math
import numpy as np
import jax
import jax.numpy as jnp
from jax import lax

D_MODEL = 1024
BATCH = 8
SEQ = 4096
DEPTH = 2

POOL_WIDTH = D_MODEL // 2
POOL_GROUPS = 4
POOL_WINDOWS = (2, 4, 8, 16)
POOL_GROUP_CH = POOL_WIDTH // POOL_GROUPS
SSM_WIDTH = D_MODEL // 2
SSM_GROUP_CH = 16
SSM_GROUPS = SSM_WIDTH // SSM_GROUP_CH
SSM_STATE = 64
NSA_HEADS = 8
NSA_KV_HEADS = 2
NSA_GROUP = NSA_HEADS // NSA_KV_HEADS
HEAD_DIM = 64
NSA_WIDTH = NSA_HEADS * HEAD_DIM
KV_WIDTH = NSA_KV_HEADS * HEAD_DIM
CMP_BLOCK = 32
CMP_STRIDE = 16
CMP_HIDDEN = 2 * HEAD_DIM
SEL_BLOCK = 64
SEL_TOPK = 16
WINDOW = 512
Q_BLOCK = 128
SEL_Q_BLOCK = 64
ROPE_THETA = 500000.0
ROPE_DIMS = HEAD_DIM // 4
NEG = -1e30
FORCE_SCORE = 1e6
D_FF = 4 * D_MODEL
ALPHA = (2 * DEPTH) ** 0.25
BETA = (8 * DEPTH) ** -0.25
LN_EPS = 1e-5
IN_SPLITS = (POOL_WIDTH, SSM_WIDTH, NSA_WIDTH, 6 * KV_WIDTH, 3 * NSA_HEADS, 3 * D_MODEL)
IN_WIDTH = sum(IN_SPLITS)

kernel_name = "hybrid_pool_s5_nsa_deepnorm"


def layer_norm(x, g, b):
    xf = x.astype(jnp.float32)
    mu = jnp.mean(xf, axis=-1, keepdims=True)
    var = jnp.mean(jnp.square(xf - mu), axis=-1, keepdims=True)
    y = (xf - mu) * lax.rsqrt(var + LN_EPS)
    return (y * g.astype(jnp.float32) + b.astype(jnp.float32)).astype(x.dtype)


def rope_tables(L):
    pos = jnp.arange(L, dtype=jnp.float32)
    inv_freq = ROPE_THETA ** (-jnp.arange(0, ROPE_DIMS, 2, dtype=jnp.float32) / ROPE_DIMS)
    ang = pos[:, None] * inv_freq[None, :]
    return jnp.cos(ang), jnp.sin(ang)


def partial_rope(x, cos, sin):
    half = ROPE_DIMS // 2
    c = cos[:, None, :].astype(x.dtype)
    s = sin[:, None, :].astype(x.dtype)
    x1 = x[..., :half]
    x2 = x[..., half:ROPE_DIMS]
    return jnp.concatenate([x1 * c - x2 * s, x1 * s + x2 * c, x[..., ROPE_DIMS:]], axis=-1)


def pool_mixer(u, w_pool, pool_scale):
    B, L, _ = u.shape
    uf = u.astype(jnp.float32)
    csum = jnp.cumsum(uf, axis=1)
    t = jnp.arange(L)
    means = []
    for gi, w in enumerate(POOL_WINDOWS):
        cs = csum[..., gi * POOL_GROUP_CH:(gi + 1) * POOL_GROUP_CH]
        lag = jnp.pad(cs[:, :L - w], ((0, 0), (w, 0), (0, 0)))
        cnt = jnp.minimum(t + 1, w).astype(jnp.float32)[None, :, None]
        means.append((cs - lag) / cnt)
    z = (jnp.concatenate(means, axis=-1) - uf).astype(u.dtype)
    z = z.reshape(B, L, POOL_GROUPS, POOL_GROUP_CH)
    y = jnp.einsum('blgc,gcd->blgd', z, w_pool).reshape(B, L, POOL_WIDTH)
    return y * pool_scale


def s5_mixer(u, lam_re, lam_im, log_dt, b_re, b_im, c_re, c_im, d_skip, w_glu, b_glu):
    B, L, _ = u.shape
    f32 = jnp.float32
    uf = u.astype(f32).reshape(B, L, SSM_GROUPS, SSM_GROUP_CH)
    lr = lam_re.astype(f32)
    li = lam_im.astype(f32)
    step = jnp.exp(log_dt.astype(f32))[:, None]
    mag = jnp.exp(lr * step)
    ab_re = mag * jnp.cos(li * step)
    ab_im = mag * jnp.sin(li * step)
    den = lr * lr + li * li
    n_re = ab_re - 1.0
    n_im = ab_im
    k_re = (n_re * lr + n_im * li) / den
    k_im = (n_im * lr - n_re * li) / den
    br = b_re.astype(f32)
    bi = b_im.astype(f32)
    bb_re = k_re[..., None] * br - k_im[..., None] * bi
    bb_im = k_re[..., None] * bi + k_im[..., None] * br
    bu_re = jnp.einsum('gpc,blgc->blgp', bb_re, uf)
    bu_im = jnp.einsum('gpc,blgc->blgp', bb_im, uf)
    a_re = jnp.broadcast_to(ab_re[None, None], (1, L) + ab_re.shape)
    a_im = jnp.broadcast_to(ab_im[None, None], (1, L) + ab_im.shape)

    def combine(e1, e2):
        a1r, a1i, b1r, b1i = e1
        a2r, a2i, b2r, b2i = e2
        return (a1r * a2r - a1i * a2i,
                a1r * a2i + a1i * a2r,
                a2r * b1r - a2i * b1i + b2r,
                a2r * b1i + a2i * b1r + b2i)

    _, _, h_re, h_im = lax.associative_scan(combine, (a_re, a_im, bu_re, bu_im), axis=1)
    y = (jnp.einsum('gcp,blgp->blgc', c_re.astype(f32), h_re)
         - jnp.einsum('gcp,blgp->blgc', c_im.astype(f32), h_im)
         + d_skip.astype(f32) * uf)
    z = jax.nn.gelu(y.reshape(B, L, SSM_WIDTH))
    out = z * jax.nn.sigmoid(z @ w_glu.astype(f32) + b_glu.astype(f32))
    return out.astype(u.dtype)


def overlap_matrix(n_cmp, n_sel):
    cs = np.arange(n_cmp)[:, None] * CMP_STRIDE
    ss = np.arange(n_sel)[None, :] * SEL_BLOCK
    ov = np.minimum(cs + CMP_BLOCK, ss + SEL_BLOCK) - np.maximum(cs, ss)
    return jnp.asarray(np.maximum(ov, 0) / CMP_STRIDE, dtype=jnp.float32)


def nsa_mixer(q, kc, vc, ks, vs, kw, vw, gate_logits, pe_k, pe_v, wk1, wk2, wv1, wv2):
    B, L = q.shape[0], q.shape[1]
    dtype = q.dtype
    f32 = jnp.float32
    scale = 1.0 / math.sqrt(HEAD_DIM)
    K, G = NSA_KV_HEADS, NSA_GROUP
    qg = q.reshape(B, L, K, G, HEAD_DIM)
    t = jnp.arange(L)

    n_cmp = (L - CMP_BLOCK) // CMP_STRIDE + 1
    idx = np.arange(n_cmp)[:, None] * CMP_STRIDE + np.arange(CMP_BLOCK)[None, :]

    def compress(src, pe, w1, w2):
        blk = src[:, idx] + pe[None, None, :, None, :]
        flat = blk.transpose(0, 1, 3, 2, 4).reshape(B, n_cmp, K, CMP_BLOCK * HEAD_DIM)
        return jax.nn.gelu(flat @ w1) @ w2

    k_cmp = compress(kc, pe_k, wk1, wk2)
    v_cmp = compress(vc, pe_v, wv1, wv2)
    s_cmp = jnp.einsum('blkgd,bnkd->bkgln', qg, k_cmp).astype(f32) * scale
    blk_end = jnp.arange(n_cmp) * CMP_STRIDE + CMP_BLOCK - 1
    cmp_valid = blk_end[None, :] <= t[:, None]
    p_cmp = jax.nn.softmax(jnp.where(cmp_valid, s_cmp, NEG), axis=-1)
    p_cmp = jnp.where(cmp_valid, p_cmp, 0.0)
    o_cmp = jnp.einsum('bkgln,bnkd->blkgd', p_cmp.astype(dtype), v_cmp)

    n_sel = L // SEL_BLOCK
    topk = min(SEL_TOPK, n_sel)
    imp = jnp.einsum('bkgln,ns->bkls', p_cmp, overlap_matrix(n_cmp, n_sel))
    j = jnp.arange(n_sel)[None, :]
    cur = (t // SEL_BLOCK)[:, None]
    forced = (j == 0) | (j == cur) | (j == cur - 1)
    sel_valid = j * SEL_BLOCK <= t[:, None]
    score = jnp.where(forced, FORCE_SCORE, jnp.where(sel_valid, imp, -FORCE_SCORE))
    _, sel_idx = lax.top_k(score, topk)

    ks_blk = ks.transpose(0, 2, 1, 3).reshape(B, K, n_sel, SEL_BLOCK, HEAD_DIM)
    vs_blk = vs.transpose(0, 2, 1, 3).reshape(B, K, n_sel, SEL_BLOCK, HEAD_DIM)
    nq = L // SEL_Q_BLOCK
    q_ch = qg.reshape(B, nq, SEL_Q_BLOCK, K, G, HEAD_DIM).transpose(1, 0, 2, 3, 4, 5)
    i_ch = sel_idx.reshape(B, K, nq, SEL_Q_BLOCK, topk).transpose(2, 0, 1, 3, 4)
    t_ch = t.reshape(nq, SEL_Q_BLOCK)
    b_ix = jnp.arange(B)[:, None, None, None]
    k_ix = jnp.arange(K)[None, :, None, None]

    def sel_chunk(args):
        qc, ic, tc = args
        kg = ks_blk[b_ix, k_ix, ic]
        vg = vs_blk[b_ix, k_ix, ic]
        s = jnp.einsum('bqkgd,bkqsjd->bkgqsj', qc, kg).astype(f32) * scale
        kpos = ic[..., None] * SEL_BLOCK + jnp.arange(SEL_BLOCK)
        mask = (kpos <= tc[None, None, :, None, None])[:, :, None]
        s = jnp.where(mask, s, NEG).reshape(B, K, G, SEL_Q_BLOCK, topk * SEL_BLOCK)
        p = jax.nn.softmax(s, axis=-1).reshape(B, K, G, SEL_Q_BLOCK, topk, SEL_BLOCK)
        return jnp.einsum('bkgqsj,bkqsjd->bqkgd', p.astype(dtype), vg)

    o_sel = lax.map(sel_chunk, (q_ch, i_ch, t_ch))
    o_sel = o_sel.transpose(1, 0, 2, 3, 4, 5).reshape(B, L, K, G, HEAD_DIM)

    kw_pad = jnp.pad(kw, ((0, 0), (WINDOW, 0), (0, 0), (0, 0)))
    vw_pad = jnp.pad(vw, ((0, 0), (WINDOW, 0), (0, 0), (0, 0)))
    nwq = L // Q_BLOCK
    qw_ch = qg.reshape(B, nwq, Q_BLOCK, K, G, HEAD_DIM).transpose(1, 0, 2, 3, 4, 5)

    def win_chunk(args):
        qc, c = args
        start = c * Q_BLOCK
        kb = lax.dynamic_slice_in_dim(kw_pad, start, WINDOW + Q_BLOCK, axis=1)
        vb = lax.dynamic_slice_in_dim(vw_pad, start, WINDOW + Q_BLOCK, axis=1)
        s = jnp.einsum('bqkgd,bskd->bkgqs', qc, kb).astype(f32) * scale
        tq = start + jnp.arange(Q_BLOCK)
        sk = start - WINDOW + jnp.arange(WINDOW + Q_BLOCK)
        diff = tq[:, None] - sk[None, :]
        mask = (sk[None, :] >= 0) & (diff >= 0) & (diff < WINDOW)
        p = jax.nn.softmax(jnp.where(mask, s, NEG), axis=-1)
        return jnp.einsum('bkgqs,bskd->bqkgd', p.astype(dtype), vb)

    o_win = lax.map(win_chunk, (qw_ch, jnp.arange(nwq)))
    o_win = o_win.transpose(1, 0, 2, 3, 4, 5).reshape(B, L, K, G, HEAD_DIM)

    g = jax.nn.sigmoid(gate_logits.astype(f32)).astype(dtype).reshape(B, L, K, G, 3)
    o = g[..., 0:1] * o_cmp + g[..., 1:2] * o_sel + g[..., 2:3] * o_win
    return o.reshape(B, L, NSA_WIDTH)


def hybrid_layer(x, cos, sin, w_in, w_pool, pool_scale, ssm_lam_re, ssm_lam_im, ssm_log_dt,
                 ssm_b_re, ssm_b_im, ssm_c_re, ssm_c_im, ssm_d, w_glu, b_glu,
                 cmp_pe_k, cmp_pe_v, cmp_wk1, cmp_wk2, cmp_wv1, cmp_wv2,
                 w_up_pool, w_up_ssm, w_up_nsa, w_out, ln1_g, ln1_b, w_ff1, w_ff2, ln2_g, ln2_b):
    B, L, _ = x.shape
    proj = x @ w_in
    offs = [int(o) for o in np.cumsum(IN_SPLITS)[:-1]]
    u_pool, u_ssm, q, kv, g_nsa, g_merge = jnp.split(proj, offs, axis=-1)
    kc, vc, ks, vs, kw, vw = [a.reshape(B, L, NSA_KV_HEADS, HEAD_DIM) for a in jnp.split(kv, 6, axis=-1)]
    q = partial_rope(q.reshape(B, L, NSA_HEADS, HEAD_DIM), cos, sin)
    kc = partial_rope(kc, cos, sin)
    ks = partial_rope(ks, cos, sin)
    kw = partial_rope(kw, cos, sin)

    y_pool = pool_mixer(u_pool, w_pool, pool_scale)
    y_ssm = s5_mixer(u_ssm, ssm_lam_re, ssm_lam_im, ssm_log_dt, ssm_b_re, ssm_b_im,
                     ssm_c_re, ssm_c_im, ssm_d, w_glu, b_glu)
    y_nsa = nsa_mixer(q, kc, vc, ks, vs, kw, vw, g_nsa.reshape(B, L, NSA_HEADS, 3),
                      cmp_pe_k, cmp_pe_v, cmp_wk1, cmp_wk2, cmp_wv1, cmp_wv2)

    gates = jax.nn.sigmoid(g_merge.astype(jnp.float32)).astype(x.dtype).reshape(B, L, 3, D_MODEL)
    merged = (gates[:, :, 0] * (y_pool @ w_up_pool)
              + gates[:, :, 1] * (y_ssm @ w_up_ssm)
              + gates[:, :, 2] * (y_nsa @ w_up_nsa))
    x = layer_norm(ALPHA * x + merged @ w_out, ln1_g, ln1_b)
    h = jnp.square(jax.nn.relu(x @ w_ff1)) @ w_ff2
    return layer_norm(ALPHA * x + h, ln2_g, ln2_b)


def setup_inputs(seed: int = 0) -> dict:
    key = jax.random.key(seed)
    k = jax.random.split(key, 40)
    f32 = jnp.float32

    def nrm(kk, shape, scale):
        return jax.random.normal(kk, shape, f32) * scale

    Dp = DEPTH
    n = jnp.arange(SSM_STATE, dtype=f32)
    return {
        "x": nrm(k[0], (BATCH, SEQ, D_MODEL), 1.0),
        "ln_in_g": 1.0 + nrm(k[1], (D_MODEL,), 0.02),
        "ln_in_b": nrm(k[2], (D_MODEL,), 0.02),
        "w_in": nrm(k[3], (Dp, D_MODEL, IN_WIDTH), D_MODEL ** -0.5),
        "w_pool": nrm(k[4], (Dp, POOL_GROUPS, POOL_GROUP_CH, POOL_GROUP_CH), POOL_GROUP_CH ** -0.5),
        "pool_scale": 1.0 + nrm(k[5], (Dp, POOL_WIDTH), 0.02),
        "ssm_lam_re": -0.5 + nrm(k[6], (Dp, SSM_GROUPS, SSM_STATE), 0.01),
        "ssm_lam_im": math.pi * n + nrm(k[7], (Dp, SSM_GROUPS, SSM_STATE), 0.01),
        "ssm_log_dt": jax.random.uniform(k[8], (Dp, SSM_GROUPS), f32, math.log(1e-3), math.log(1e-1)),
        "ssm_b_re": nrm(k[9], (Dp, SSM_GROUPS, SSM_STATE, SSM_GROUP_CH), (2 * SSM_GROUP_CH) ** -0.5),
        "ssm_b_im": nrm(k[10], (Dp, SSM_GROUPS, SSM_STATE, SSM_GROUP_CH), (2 * SSM_GROUP_CH) ** -0.5),
        "ssm_c_re": nrm(k[11], (Dp, SSM_GROUPS, SSM_GROUP_CH, SSM_STATE), SSM_STATE ** -0.5),
        "ssm_c_im": nrm(k[12], (Dp, SSM_GROUPS, SSM_GROUP_CH, SSM_STATE), SSM_STATE ** -0.5),
        "ssm_d": nrm(k[13], (Dp, SSM_GROUPS, SSM_GROUP_CH), 1.0),
        "w_glu": nrm(k[14], (Dp, SSM_WIDTH, SSM_WIDTH), SSM_WIDTH ** -0.5),
        "b_glu": nrm(k[15], (Dp, SSM_WIDTH), 0.02),
        "cmp_pe_k": nrm(k[16], (Dp, CMP_BLOCK, HEAD_DIM), 0.1),
        "cmp_pe_v": nrm(k[17], (Dp, CMP_BLOCK, HEAD_DIM), 0.1),
        "cmp_wk1": nrm(k[18], (Dp, CMP_BLOCK * HEAD_DIM, CMP_HIDDEN), (CMP_BLOCK * HEAD_DIM) ** -0.5),
        "cmp_wk2": nrm(k[19], (Dp, CMP_HIDDEN, HEAD_DIM), CMP_HIDDEN ** -0.5),
        "cmp_wv1": nrm(k[20], (Dp, CMP_BLOCK * HEAD_DIM, CMP_HIDDEN), (CMP_BLOCK * HEAD_DIM) ** -0.5),
        "cmp_wv2": nrm(k[21], (Dp, CMP_HIDDEN, HEAD_DIM), CMP_HIDDEN ** -0.5),
        "w_up_pool": nrm(k[22], (Dp, POOL_WIDTH, D_MODEL), POOL_WIDTH ** -0.5),
        "w_up_ssm": nrm(k[23], (Dp, SSM_WIDTH, D_MODEL), SSM_WIDTH ** -0.5),
        "w_up_nsa": nrm(k[24], (Dp, NSA_WIDTH, D_MODEL), NSA_WIDTH ** -0.5),
        "w_out": nrm(k[25], (Dp, D_MODEL, D_MODEL), BETA * D_MODEL ** -0.5),
        "ln1_g": 1.0 + nrm(k[26], (Dp, D_MODEL), 0.02),
        "ln1_b": nrm(k[27], (Dp, D_MODEL), 0.02),
        "w_ff1": nrm(k[28], (Dp, D_MODEL, D_FF), D_MODEL ** -0.5),
        "w_ff2": nrm(k[29], (Dp, D_FF, D_MODEL), BETA * D_FF ** -0.5),
        "ln2_g": 1.0 + nrm(k[30], (Dp, D_MODEL), 0.02),
        "ln2_b": nrm(k[31], (Dp, D_MODEL), 0.02),
    }


def reference(x, ln_in_g, ln_in_b, w_in, w_pool, pool_scale, ssm_lam_re, ssm_lam_im, ssm_log_dt,
              ssm_b_re, ssm_b_im, ssm_c_re, ssm_c_im, ssm_d, w_glu, b_glu,
              cmp_pe_k, cmp_pe_v, cmp_wk1, cmp_wk2, cmp_wv1, cmp_wv2,
              w_up_pool, w_up_ssm, w_up_nsa, w_out, ln1_g, ln1_b, w_ff1, w_ff2, ln2_g, ln2_b):
    cos, sin = rope_tables(x.shape[1])
    x = layer_norm(x, ln_in_g, ln_in_b)
    for i in range(DEPTH):
        x = hybrid_layer(x, cos, sin, w_in[i], w_pool[i], pool_scale[i], ssm_lam_re[i], ssm_lam_im[i],
                         ssm_log_dt[i], ssm_b_re[i], ssm_b_im[i], ssm_c_re[i], ssm_c_im[i], ssm_d[i],
                         w_glu[i], b_glu[i], cmp_pe_k[i], cmp_pe_v[i], cmp_wk1[i], cmp_wk2[i],
                         cmp_wv1[i], cmp_wv2[i], w_up_pool[i], w_up_ssm[i], w_up_nsa[i], w_out[i],
                         ln1_g[i], ln1_b[i], w_ff1[i], w_ff2[i], ln2_g[i], ln2_b[i])
    return x
```

```python
import functools
import math

import numpy as np
import jax
import jax.numpy as jnp
from jax import lax
from jax.experimental import pallas as pl
from jax.experimental.pallas import tpu as pltpu

D_MODEL = 1024
DEPTH = 2
POOL_WIDTH = D_MODEL // 2
POOL_GROUPS = 4
POOL_WINDOWS = (2, 4, 8, 16)
POOL_GROUP_CH = POOL_WIDTH // POOL_GROUPS
SSM_WIDTH = D_MODEL // 2
SSM_GROUP_CH = 16
SSM_GROUPS = SSM_WIDTH // SSM_GROUP_CH
SSM_STATE = 64
NSA_HEADS = 8
NSA_KV_HEADS = 2
NSA_GROUP = NSA_HEADS // NSA_KV_HEADS
HEAD_DIM = 64
NSA_WIDTH = NSA_HEADS * HEAD_DIM
KV_WIDTH = NSA_KV_HEADS * HEAD_DIM
CMP_BLOCK = 32
CMP_STRIDE = 16
CMP_HIDDEN = 2 * HEAD_DIM
SEL_BLOCK = 64
SEL_TOPK = 16
WINDOW = 512
ROPE_THETA = 500000.0
ROPE_DIMS = HEAD_DIM // 4
ROPE_HALF = ROPE_DIMS // 2
NEG = -1e30
FORCE_SCORE = 1e6
D_FF = 4 * D_MODEL
ALPHA = (2 * DEPTH) ** 0.25
LN_EPS = 1e-5

LANES = 128
SUBLANES = 8
VMEM_LIMIT = 56 * 1024 * 1024

MXU_DTYPE = jnp.bfloat16
F32 = jnp.float32

SEL_BIAS = -(2.0 ** 100)
SSM_CHUNK = 16
MAX_SEL_BLOCKS = 64

ROW_TILE = 512
Q_TILE = 128
SEL_KEY_TILE = 512


def _nt_dot(a, b):
    return lax.dot_general(a, b, (((1,), (1,)), ((), ())), preferred_element_type=F32)


def _dot(a, b):
    return jnp.dot(a, b, preferred_element_type=F32)


def _split_dot(a, b, nt=False):
    f = _nt_dot if nt else _dot
    hi = a.astype(MXU_DTYPE)
    if MXU_DTYPE == jnp.float32:
        return (f(b, hi) if nt else f(hi, b))
    lo = (a - hi.astype(F32)).astype(MXU_DTYPE)
    if nt:
        return f(b, hi) + f(b, lo)
    return f(hi, b) + f(lo, b)


def _gelu_tanh(x):
    return x * (0.5 * (1.0 + jnp.tanh(math.sqrt(2.0 / math.pi) * (x + 0.044715 * (x * x * x)))))


def _layer_norm(xf, g, b):
    mu = jnp.mean(xf, axis=-1, keepdims=True)
    xc = xf - mu
    var = jnp.mean(xc * xc, axis=-1, keepdims=True)
    return xc * lax.rsqrt(var + LN_EPS) * g + b


def _params(*sem):
    return pltpu.CompilerParams(dimension_semantics=sem, vmem_limit_bytes=VMEM_LIMIT)


def _const_spec(shape):
    nd = len(shape)
    return pl.BlockSpec(shape, lambda *_: (0,) * nd, pipeline_mode=pl.Buffered(1))


def _ln_kernel(x_ref, g_ref, b_ref, o_ref):
    o_ref[...] = _layer_norm(x_ref[...], g_ref[...], b_ref[...])


def _entry_norm(x2, g, b):
    n, d = x2.shape
    tm = ROW_TILE
    return pl.pallas_call(
        _ln_kernel,
        grid=(n // tm,),
        in_specs=[pl.BlockSpec((tm, d), lambda i: (i, 0)), _const_spec((1, d)), _const_spec((1, d))],
        out_specs=pl.BlockSpec((tm, d), lambda i: (i, 0)),
        out_shape=jax.ShapeDtypeStruct((n, d), F32),
        compiler_params=_params("parallel"),
        name="entry_norm",
    )(x2, g.reshape(1, d), b.reshape(1, d))


_SEC = {}
_off = 0
for _name, _w in (("pool", 512), ("ssm", 512), ("q", 512), ("kse", 256), ("kso", 256), ("kwd", 256),
                  ("kc", 128), ("vsd", 256), ("vwd", 256), ("vc", 128), ("gn", 128)):
    _SEC[_name] = (_off, _off + _w)
    _off += _w
IN_COLS = _off


def _inproj_kernel(x_ref, w_ref, cs_ref, s1_ref, s2_ref, ohe_ref, oho_ref,
                   up_ref, us_ref, q_ref, kse_ref, kso_ref, kwd_ref, kc_ref, vsd_ref, vwd_ref, vc_ref, gn_ref):
    xb = x_ref[...].astype(MXU_DTYPE)

    def proj(name):
        lo, hi = _SEC[name]
        return _dot(xb, w_ref[:, lo:hi])

    cs, s1, s2 = cs_ref[...], s1_ref[...], s2_ref[...]

    def rope(a):
        return a * cs + pltpu.roll(a, LANES - ROPE_HALF, 1) * s1 + pltpu.roll(a, ROPE_HALF, 1) * s2

    def store_roped(name, ref, add=None):
        acc = proj(name)
        for j in range(acc.shape[1] // LANES):
            t = rope(acc[:, j * LANES:(j + 1) * LANES])
            if add is not None:
                t = t + add
            ref[:, j * LANES:(j + 1) * LANES] = t.astype(ref.dtype)

    up_ref[...] = proj("pool")
    us_ref[...] = proj("ssm").astype(us_ref.dtype)
    store_roped("q", q_ref)
    store_roped("kse", kse_ref, ohe_ref[...])
    store_roped("kso", kso_ref, oho_ref[...])
    store_roped("kwd", kwd_ref)
    store_roped("kc", kc_ref)
    vsd_ref[...] = proj("vsd").astype(vsd_ref.dtype)
    vwd_ref[...] = proj("vwd").astype(vwd_ref.dtype)
    vc_ref[...] = proj("vc")
    gn_ref[...] = jax.nn.sigmoid(proj("gn"))


def _in_projection(xn, w_cat, tabs, L):
    n, d = xn.shape
    tm = ROW_TILE
    nl = L // tm
    row = lambda w: pl.BlockSpec((tm, w), lambda i: (i, 0))
    tab = pl.BlockSpec((tm, LANES), lambda i: (i % nl, 0))
    bf = MXU_DTYPE
    outs = [("pool", F32), ("ssm", bf), ("q", bf), ("kse", bf), ("kso", bf), ("kwd", bf), ("kc", F32),
            ("vsd", bf), ("vwd", bf), ("vc", F32), ("gn", F32)]
    widths = [_SEC[k][1] - _SEC[k][0] for k, _ in outs]
    return pl.pallas_call(
        _inproj_kernel,
        grid=(n // tm,),
        in_specs=[row(d), _const_spec((d, IN_COLS)), tab, tab, tab, tab, tab],
        out_specs=[row(w) for w in widths],
        out_shape=[jax.ShapeDtypeStruct((n, w), dt) for w, (_, dt) in zip(widths, outs)],
        compiler_params=_params("parallel"),
        name="in_projection",
    )(xn, w_cat, *tabs)


def _rope_tables(L):
    pos = jnp.arange(L, dtype=F32)
    inv_freq = ROPE_THETA ** (-jnp.arange(0, ROPE_DIMS, 2, dtype=F32) / ROPE_DIMS)
    ang = pos[:, None] * inv_freq[None, :]
    cos, sin = jnp.cos(ang), jnp.sin(ang)
    ones = jnp.ones((L, HEAD_DIM - ROPE_DIMS), F32)
    zeros = jnp.zeros((L, HEAD_DIM - ROPE_DIMS), F32)
    z8 = jnp.zeros((L, ROPE_HALF), F32)
    c64 = jnp.concatenate([cos, cos, ones], axis=1)
    s1_64 = jnp.concatenate([-sin, z8, zeros], axis=1)
    s2_64 = jnp.concatenate([z8, sin, zeros], axis=1)
    dup = lambda a: jnp.concatenate([a, a], axis=1)
    onehot = (jnp.arange(L)[:, None] // SEL_BLOCK == jnp.arange(MAX_SEL_BLOCKS)[None, :]).astype(F32)
    z64 = jnp.zeros((L, HEAD_DIM), F32)
    oh_e = jnp.concatenate([z64, onehot], axis=1)
    oh_o = jnp.concatenate([onehot, z64], axis=1)
    return dup(c64), dup(s1_64), dup(s2_64), oh_e, oh_o


def _expand_in_weight(w_in):
    o = np.cumsum((0, POOL_WIDTH, SSM_WIDTH, NSA_WIDTH))
    w_pool, w_ssm = w_in[:, o[0]:o[1]], w_in[:, o[1]:o[2]]
    w_q = w_in[:, o[2]:o[3]] * (1.0 / math.sqrt(HEAD_DIM))
    kv0 = int(o[3])
    kvs = [w_in[:, kv0 + i * KV_WIDTH: kv0 + (i + 1) * KV_WIDTH] for i in range(6)]
    kc, vc, ks, vs, kw, vw = kvs
    head = lambda a, k: a[:, k * HEAD_DIM:(k + 1) * HEAD_DIM]
    z = jnp.zeros((w_in.shape[0], HEAD_DIM), w_in.dtype)
    kse = jnp.concatenate([head(ks, 0), z, head(ks, 1), z], axis=1)
    kso = jnp.concatenate([z, head(ks, 0), z, head(ks, 1)], axis=1)
    dupl = lambda a: jnp.concatenate([head(a, 0), head(a, 0), head(a, 1), head(a, 1)], axis=1)
    g0 = kv0 + 6 * KV_WIDTH
    gn = w_in[:, g0:g0 + 3 * NSA_HEADS]
    gn = jnp.pad(gn, ((0, 0), (0, LANES - 3 * NSA_HEADS)))
    w_cat = jnp.concatenate([w_pool, w_ssm, w_q, kse, kso, dupl(kw), kc, dupl(vs), dupl(vw), vc, gn], axis=1)
    w_gate = w_in[:, g0 + 3 * NSA_HEADS:]
    return w_cat.astype(MXU_DTYPE), w_gate.astype(MXU_DTYPE)


def _s5_matrices(lam_re, lam_im, log_dt, b_re, b_im, c_re, c_im, d_skip):
    T = SSM_CHUNK
    hp = lax.Precision.HIGHEST
    step = jnp.exp(log_dt)[None, :, None]
    n = jnp.arange(T + 1, dtype=F32)[:, None, None]
    mag = jnp.exp(n * lam_re[None] * step)
    ang = n * lam_im[None] * step
    pr, pi = mag * jnp.cos(ang), mag * jnp.sin(ang)
    den = lam_re * lam_re + lam_im * lam_im
    n_re, n_im = pr[1] - 1.0, pi[1]
    k_re = (n_re * lam_re + n_im * lam_im) / den
    k_im = (n_im * lam_re - n_re * lam_im) / den
    bb_re = k_re[..., None] * b_re - k_im[..., None] * b_im
    bb_im = k_re[..., None] * b_im + k_im[..., None] * b_re
    ca_re = c_re[None] * pr[:, :, None, :] - c_im[None] * pi[:, :, None, :]
    ca_im = c_re[None] * pi[:, :, None, :] + c_im[None] * pr[:, :, None, :]
    taps = (jnp.einsum('tgcp,gpd->tgcd', ca_re[:T], bb_re, precision=hp)
            - jnp.einsum('tgcp,gpd->tgcd', ca_im[:T], bb_im, precision=hp))
    eye = jnp.eye(SSM_GROUP_CH, dtype=F32)
    taps = taps.at[0].add(d_skip[:, :, None] * eye[None])
    s_idx = jnp.arange(T)[:, None]
    t_idx = jnp.arange(T)[None, :]
    lag = t_idx - s_idx
    toe = taps[jnp.clip(lag, 0, T - 1)] * (lag >= 0)[:, :, None, None, None].astype(F32)
    w_toe = toe.transpose(2, 0, 4, 1, 3).reshape(SSM_GROUPS, T * SSM_GROUP_CH, T * SSM_GROUP_CH)
    rev_r, rev_i = pr[:T][::-1], pi[:T][::-1]
    bc_re = rev_r[..., None] * bb_re[None] - rev_i[..., None] * bb_im[None]
    bc_im = rev_r[..., None] * bb_im[None] + rev_i[..., None] * bb_re[None]
    to_in = lambda a: a.transpose(1, 0, 3, 2).reshape(SSM_GROUPS, T * SSM_GROUP_CH, SSM_STATE)
    to_out = lambda a: a.transpose(1, 3, 0, 2).reshape(SSM_GROUPS, SSM_STATE, T * SSM_GROUP_CH)
    cc_re, cc_im = to_out(ca_re[1:]), to_out(-ca_im[1:])
    a_t = (pr[T].reshape(SSM_GROUPS, 1, SSM_STATE), pi[T].reshape(SSM_GROUPS, 1, SSM_STATE))
    cast = lambda a: a.astype(MXU_DTYPE)
    return cast(w_toe), cast(to_in(bc_re)), cast(to_in(bc_im)), cast(cc_re), cast(cc_im), a_t[0], a_t[1]


def _s5_kernel(u_ref, wt_ref, bcr_ref, bci_ref, ccr_ref, cci_ref, ar_ref, ai_ref, z_ref,
               sr_ref, si_ref, hr_ref, hi_ref, *, nchunk, bp):
    u = u_ref[...]
    sr_ref[...] = _dot(u, bcr_ref[...])
    si_ref[...] = _dot(u, bci_ref[...])
    ar = jnp.broadcast_to(ar_ref[...], (bp, SSM_STATE))
    ai = jnp.broadcast_to(ai_ref[...], (bp, SSM_STATE))

    def body(c, carry):
        hr, hi = carry
        rows = pl.ds(pl.multiple_of(c * bp, bp), bp)
        hr_ref[rows, :] = hr
        hi_ref[rows, :] = hi
        return (ar * hr - ai * hi + sr_ref[rows, :], ar * hi + ai * hr + si_ref[rows, :])

    zero = jnp.zeros((bp, SSM_STATE), F32)
    lax.fori_loop(0, nchunk, body, (zero, zero))
    y = (_dot(u, wt_ref[...])
         + _dot(hr_ref[...].astype(MXU_DTYPE), ccr_ref[...])
         + _dot(hi_ref[...].astype(MXU_DTYPE), cci_ref[...]))
    z_ref[...] = _gelu_tanh(y).astype(z_ref.dtype)


def _s5_mixer(u_ssm, mats, B, L):
    T, G, C = SSM_CHUNK, SSM_GROUPS, SSM_GROUP_CH
    nchunk = L // T
    bp = -(-B // SUBLANES) * SUBLANES
    u5 = u_ssm.reshape(B, nchunk, T, G, C)
    if bp != B:
        u5 = jnp.pad(u5, ((0, bp - B), (0, 0), (0, 0), (0, 0), (0, 0)))
    u_r = u5.transpose(3, 1, 0, 2, 4).reshape(G, nchunk * bp, T * C)
    rows, tc = nchunk * bp, T * C
    grp = lambda *s: pl.BlockSpec((None,) + s, lambda g: (g,) + (0,) * len(s))
    z_r = pl.pallas_call(
        functools.partial(_s5_kernel, nchunk=nchunk, bp=bp),
        grid=(G,),
        in_specs=[grp(rows, tc), grp(tc, tc), grp(tc, SSM_STATE), grp(tc, SSM_STATE),
                  grp(SSM_STATE, tc), grp(SSM_STATE, tc), grp(1, SSM_STATE), grp(1, SSM_STATE)],
        out_specs=grp(rows, tc),
        out_shape=jax.ShapeDtypeStruct((G, rows, tc), MXU_DTYPE),
        scratch_shapes=[pltpu.VMEM((rows, SSM_STATE), F32) for _ in range(4)],
        compiler_params=_params("parallel"),
        name="s5_mixer",
    )(u_r, *mats)
    z5 = z_r.reshape(G, nchunk, bp, T, C)[:, :, :B]
    return z5.transpose(2, 1, 3, 0, 4).reshape(B * L, G * C)


def _compress_kernel(kc_ref, vc_ref, pek_ref, pev_ref, wk1a_ref, wk1b_ref, wk2_ref, wv1a_ref, wv1b_ref, wv2_ref,
                     ko_ref, vo_ref):
    def one(src_ref, pe_ref, w1a_ref, w1b_ref, w2_ref, out_ref):
        r = src_ref[...]
        nrow = r.shape[0]
        first = (r + pe_ref[0:1, :]).astype(MXU_DTYPE)
        second = (r + pe_ref[1:2, :]).astype(MXU_DTYPE)
        hid = _dot(first, w1a_ref[...]) + pltpu.roll(_dot(second, w1b_ref[...]), nrow - 1, 0)
        act = _gelu_tanh(hid).astype(MXU_DTYPE)
        for k in range(NSA_KV_HEADS):
            out_ref[k] = _dot(act, w2_ref[k]).astype(out_ref.dtype)

    one(kc_ref, pek_ref, wk1a_ref, wk1b_ref, wk2_ref, ko_ref)
    one(vc_ref, pev_ref, wv1a_ref, wv1b_ref, wv2_ref, vo_ref)


def _compress_weights(pe, w1, w2):
    half = CMP_STRIDE
    w1r = w1.reshape(CMP_BLOCK, HEAD_DIM, CMP_HIDDEN)
    eye = jnp.eye(NSA_KV_HEADS, dtype=w1.dtype)

    def expand(part):
        e = jnp.einsum('tdj,hg->thdgj', part, eye)
        return e.reshape(half * NSA_KV_HEADS * HEAD_DIM, NSA_KV_HEADS * CMP_HIDDEN)

    w1a, w1b = expand(w1r[:half]), expand(w1r[half:])
    pe2 = jnp.broadcast_to(pe.reshape(2, half, 1, HEAD_DIM), (2, half, NSA_KV_HEADS, HEAD_DIM))
    pe2 = pe2.reshape(2, half * NSA_KV_HEADS * HEAD_DIM)
    w2k = []
    for k in range(NSA_KV_HEADS):
        blk = jnp.zeros((NSA_KV_HEADS, CMP_HIDDEN, 2, HEAD_DIM), w2.dtype)
        blk = blk.at[k].set(jnp.broadcast_to(w2[:, None, :], (CMP_HIDDEN, 2, HEAD_DIM)))
        w2k.append(blk.reshape(NSA_KV_HEADS * CMP_HIDDEN, 2 * HEAD_DIM))
    return pe2.astype(F32), w1a.astype(MXU_DTYPE), w1b.astype(MXU_DTYPE), jnp.stack(w2k).astype(MXU_DTYPE)


def _compress(kc, vc, wk, wv, B, L):
    nrow = L // CMP_STRIDE
    width = CMP_STRIDE * NSA_KV_HEADS * HEAD_DIM
    src = pl.BlockSpec((None, nrow, width), lambda b: (b, 0, 0))
    out = pl.BlockSpec((None, NSA_KV_HEADS, nrow, LANES), lambda b: (b, 0, 0, 0))
    wspecs = lambda ws: [_const_spec(w.shape) for w in ws]
    pek, wk1a, wk1b, wk2 = wk
    pev, wv1a, wv1b, wv2 = wv
    shape = jax.ShapeDtypeStruct((B, NSA_KV_HEADS, nrow, LANES), MXU_DTYPE)
    return pl.pallas_call(
        _compress_kernel,
        grid=(B,),
        in_specs=[src, src] + wspecs([pek, pev, wk1a, wk1b, wk2, wv1a, wv1b, wv2]),
        out_specs=[out, out],
        out_shape=[shape, shape],
        compiler_params=_params("parallel"),
        name="kv_compress",
    )(kc.reshape(B, nrow, width), vc.reshape(B, nrow, width), pek, pev, wk1a, wk1b, wk2, wv1a, wv1b, wv2)


def _nsa_kernel(q_ref, gn_ref, kcmp_ref, vcmp_ref, kse_ref, kso_ref, vsd_ref, kwd_ref, vwd_ref, ovt_ref, gexp_ref,
                o_ref, *, tq, tkb, ncmp, topk):
    q0 = pl.multiple_of(pl.program_id(1) * tq, tq)
    lo = lax.broadcasted_iota(jnp.int32, (tq, LANES), 1) < HEAD_DIM
    qt = [q_ref[:, j * LANES:(j + 1) * LANES] for j in range(NSA_HEADS // 2)]
    qzero = jnp.zeros_like(qt[0])

    def even(tile, fill):
        return jnp.where(lo, tile, fill)

    def odd(tile, fill):
        return jnp.where(lo, fill, tile)

    def stack(fn, k, fill):
        return jnp.concatenate([fn(qt[2 * k], fill), fn(qt[2 * k + 1], fill)], axis=0)

    def softmax_pv(s, valid, v):
        sm = jnp.where(valid, s, NEG)
        m = jnp.max(sm, axis=-1, keepdims=True)
        e = jnp.exp(sm - m)
        den = jnp.sum(e, axis=-1, keepdims=True)
        return _dot(e.astype(MXU_DTYPE), v) / den

    t_c = q0 + lax.broadcasted_iota(jnp.int32, (tq, ncmp), 0)
    n_c = lax.broadcasted_iota(jnp.int32, (tq, ncmp), 1)
    cvalid = n_c * CMP_STRIDE + (CMP_BLOCK - 1) <= t_c
    o_cmp = [None] * NSA_HEADS
    bias = [None] * NSA_KV_HEADS
    j_s = lax.broadcasted_iota(jnp.int32, (MAX_SEL_BLOCKS, tq), 0)
    cur = (q0 + lax.broadcasted_iota(jnp.int32, (MAX_SEL_BLOCKS, tq), 1)) // SEL_BLOCK
    forced = (j_s == 0) | (j_s == cur) | (j_s == cur - 1)
    for k in range(NSA_KV_HEADS):
        kcd, vcd = kcmp_ref[k], vcmp_ref[k]
        psum = jnp.zeros((tq, ncmp), F32)
        for g in range(NSA_GROUP):
            h = NSA_GROUP * k + g
            qh = (even if h % 2 == 0 else odd)(qt[h // 2], qzero)
            sm = jnp.where(cvalid, _nt_dot(qh, kcd), NEG)
            m = jnp.max(sm, axis=-1, keepdims=True)
            e = jnp.where(cvalid, jnp.exp(sm - m), 0.0)
            den = jnp.sum(e, axis=-1, keepdims=True)
            p = e / jnp.where(den > 0.0, den, 1.0)
            o_cmp[h] = _dot(p.astype(MXU_DTYPE), vcd)
            psum = psum + p
        imp_t = _split_dot(psum, ovt_ref[...], nt=True)
        score = jnp.where(forced, FORCE_SCORE, jnp.where(j_s <= cur, imp_t, -FORCE_SCORE))
        nblk = MAX_SEL_BLOCKS // SUBLANES
        sblk = [score[r * SUBLANES:(r + 1) * SUBLANES] for r in range(nblk)]
        jblk = j_s[0:SUBLANES]
        cnt = [jnp.zeros((SUBLANES, tq), F32) for _ in range(nblk)]
        for i in range(MAX_SEL_BLOCKS):
            ri = jnp.broadcast_to(score[i:i + 1, :], (SUBLANES, tq))
            for r in range(nblk):
                if r * SUBLANES > i:
                    beats = ri >= sblk[r]
                elif r * SUBLANES + SUBLANES - 1 < i:
                    beats = ri > sblk[r]
                else:
                    tie = jnp.where(jblk + r * SUBLANES > i, 1.0, 0.0)
                    beats = jnp.where(ri > sblk[r], 1.0, jnp.where(ri == sblk[r], tie, 0.0)) > 0.5
                cnt[r] = cnt[r] + jnp.where(beats, 1.0, 0.0)
        drop = jnp.concatenate([jnp.where(c < float(topk), 0.0, 1.0) for c in cnt] * 2, axis=0)
        bias[k] = (jnp.transpose(drop) * SEL_BIAS).astype(MXU_DTYPE)

    rows2 = 2 * tq
    t_d = q0 + lax.broadcasted_iota(jnp.int32, (tq, tkb), 0)
    t_d = jnp.concatenate([t_d, t_d], axis=0)
    col_d = lax.broadcasted_iota(jnp.int32, (rows2, tkb), 1)
    nfull = q0 // tkb

    def flash(qs, kref, vref, k):
        lanes = slice(k * LANES, (k + 1) * LANES)

        def step(kb, carry, diagonal):
            m, l, acc = carry
            rows = pl.ds(pl.multiple_of(kb * tkb, tkb), tkb)
            s = _nt_dot(qs, kref[rows, lanes])
            if diagonal:
                s = jnp.where(col_d + kb * tkb <= t_d, s, NEG)
            m_new = jnp.maximum(m, jnp.max(s, axis=-1, keepdims=True))
            alpha = jnp.exp(m - m_new)
            p = jnp.exp(s - m_new)
            l = alpha * l + jnp.sum(p, axis=-1, keepdims=True)
            acc = alpha * acc + _dot(p.astype(MXU_DTYPE), vref[rows, lanes])
            return m_new, l, acc

        init = (jnp.full((rows2, 1), NEG, F32), jnp.zeros((rows2, 1), F32), jnp.zeros((rows2, LANES), F32))
        carry = lax.fori_loop(0, nfull, lambda kb, c: step(kb, c, False), init)
        _, l, acc = step(nfull, carry, True)
        return acc / l

    span = WINDOW + tq
    start = pl.multiple_of(jnp.maximum(q0 - WINDOW, 0), tq)
    t_w = q0 + lax.broadcasted_iota(jnp.int32, (tq, span), 0)
    k_w = start + lax.broadcasted_iota(jnp.int32, (tq, span), 1)
    wvalid = (k_w <= t_w) & (t_w - k_w < WINDOW)
    wvalid = jnp.concatenate([wvalid, wvalid], axis=0)

    gates = _split_dot(gn_ref[...], gexp_ref[...])

    for k in range(NSA_KV_HEADS):
        lanes = slice(k * LANES, (k + 1) * LANES)
        sel_e = flash(stack(even, k, bias[k]), kse_ref, vsd_ref, k)
        sel_o = flash(stack(odd, k, bias[k]), kso_ref, vsd_ref, k)
        kw = kwd_ref[pl.ds(start, span), lanes]
        vw = vwd_ref[pl.ds(start, span), lanes]
        win_e = softmax_pv(_nt_dot(stack(even, k, qzero), kw), wvalid, vw)
        win_o = softmax_pv(_nt_dot(stack(odd, k, qzero), kw), wvalid, vw)
        for half in range(2):
            j = 2 * k + half
            r = slice(half * tq, (half + 1) * tq)
            cmp_t = jnp.where(lo, o_cmp[2 * j], o_cmp[2 * j + 1])
            sel_t = jnp.where(lo, sel_e[r], sel_o[r])
            win_t = jnp.where(lo, win_e[r], win_o[r])
            gl = lambda br: gates[:, br * NSA_WIDTH + j * LANES: br * NSA_WIDTH + (j + 1) * LANES]
            o_ref[:, j * LANES:(j + 1) * LANES] = (gl(0) * cmp_t + gl(1) * sel_t + gl(2) * win_t).astype(o_ref.dtype)


def _overlap_t(ncmp, n_sel):
    cs = np.arange(ncmp)[None, :] * CMP_STRIDE
    ss = np.arange(MAX_SEL_BLOCKS)[:, None] * SEL_BLOCK
    ov = np.minimum(cs + CMP_BLOCK, ss + SEL_BLOCK) - np.maximum(cs, ss)
    ov = np.maximum(ov, 0) / CMP_STRIDE
    ov[n_sel:] = 0.0
    ov[:, (ncmp - 1):] = 0.0
    return jnp.asarray(ov, dtype=MXU_DTYPE)


def _gate_expand():
    e = np.zeros((LANES, 3 * NSA_WIDTH), np.float32)
    for h in range(NSA_HEADS):
        for br in range(3):
            e[h * 3 + br, br * NSA_WIDTH + h * HEAD_DIM: br * NSA_WIDTH + (h + 1) * HEAD_DIM] = 1.0
    return jnp.asarray(e, dtype=MXU_DTYPE)


def _nsa_attention(q2, gn, kcmp, vcmp, kse, kso, vsd, kwd, vwd, B, L):
    tq = Q_TILE
    tkb = min(SEL_KEY_TILE, L)
    ncmp = L // CMP_STRIDE
    n_sel = L // SEL_BLOCK
    assert n_sel <= MAX_SEL_BLOCKS and L >= WINDOW + tq and L % tkb == 0 and tkb % tq == 0
    nq = L // tq
    seq = lambda a: a.reshape(B, L, a.shape[-1])
    kv_spec = pl.BlockSpec((None, L, 2 * LANES), lambda b, i: (b, 0, 0))
    cmp_spec = pl.BlockSpec((None, NSA_KV_HEADS, ncmp, LANES), lambda b, i: (b, 0, 0, 0))
    row = lambda w: pl.BlockSpec((tq, w), lambda b, i: (b * nq + i, 0))
    ovt, gexp = _overlap_t(ncmp, n_sel), _gate_expand()
    return pl.pallas_call(
        functools.partial(_nsa_kernel, tq=tq, tkb=tkb, ncmp=ncmp, topk=min(SEL_TOPK, n_sel)),
        grid=(B, nq),
        in_specs=[row(NSA_WIDTH), row(LANES), cmp_spec, cmp_spec, kv_spec, kv_spec, kv_spec, kv_spec, kv_spec,
                  _const_spec(ovt.shape), _const_spec(gexp.shape)],
        out_specs=row(NSA_WIDTH),
        out_shape=jax.ShapeDtypeStruct((B * L, NSA_WIDTH), MXU_DTYPE),
        compiler_params=_params("parallel", "arbitrary"),
        name="nsa_attention",
    )(q2, gn, kcmp, vcmp, seq(kse), seq(kso), seq(vsd), seq(kwd), seq(vwd), ovt, gexp)


POOL_HALO = 16


def _merge_kernel(x_ref, up_ref, halo_ref, zs_ref, yn_ref, wg_ref, wpool_ref, pscale_ref, wglu_ref, bglu_ref,
                  wup_p_ref, wup_s_ref, wup_n_ref, wout_ref, g_ref, b_ref, o_ref, *, tm, nl):
    x = x_ref[...]
    xb = x.astype(MXU_DTYPE)
    first_tile = (pl.program_id(0) % nl) == 0
    halo = jnp.where(first_tile, 0.0, halo_ref[...])
    ext = jnp.concatenate([halo, up_ref[...]], axis=0)
    t_in_seq = (pl.program_id(0) % nl) * tm + lax.broadcasted_iota(jnp.int32, (tm, 1), 0)
    y_pool = []
    for gi, w in enumerate(POOL_WINDOWS):
        lanes = slice(gi * POOL_GROUP_CH, (gi + 1) * POOL_GROUP_CH)
        e = ext[:, lanes]
        span = 1
        while span < w:
            e = e + pltpu.roll(e, span, 0)
            span *= 2
        cnt = jnp.minimum(t_in_seq + 1, w).astype(F32)
        z = e[POOL_HALO:] / cnt - ext[POOL_HALO:, lanes]
        y_pool.append(_dot(z.astype(MXU_DTYPE), wpool_ref[gi]))
    y_pool = jnp.concatenate(y_pool, axis=1) * pscale_ref[...]
    zs = zs_ref[...]
    y_ssm = zs.astype(F32) * jax.nn.sigmoid(_dot(zs, wglu_ref[...]) + bglu_ref[...])
    d = x.shape[1]
    merged = jnp.zeros_like(x)
    for br, (y, w_ref) in enumerate(((y_pool.astype(MXU_DTYPE), wup_p_ref), (y_ssm.astype(MXU_DTYPE), wup_s_ref),
                                     (yn_ref[...], wup_n_ref))):
        gate = jax.nn.sigmoid(_dot(xb, wg_ref[:, br * d:(br + 1) * d]))
        merged = merged + gate * _dot(y, w_ref[...])
    o_ref[...] = _layer_norm(ALPHA * x + _dot(merged.astype(MXU_DTYPE), wout_ref[...]), g_ref[...], b_ref[...])


def _merge(xn, u_pool, z_ssm, y_nsa, w, L):
    n, d = xn.shape
    tm = ROW_TILE // 2
    nl = L // tm
    row = lambda wd: pl.BlockSpec((tm, wd), lambda i: (i, 0))
    halo = pl.BlockSpec((POOL_HALO, POOL_WIDTH), lambda i: (jnp.maximum(i * (tm // POOL_HALO) - 1, 0), 0))
    consts = [w["w_gate"], w["w_pool"], w["pool_scale"], w["w_glu"], w["b_glu"], w["w_up_pool"], w["w_up_ssm"],
              w["w_up_nsa"], w["w_out"], w["ln1_g"], w["ln1_b"]]
    return pl.pallas_call(
        functools.partial(_merge_kernel, tm=tm, nl=nl),
        grid=(n // tm,),
        in_specs=[row(d), row(POOL_WIDTH), halo, row(SSM_WIDTH), row(NSA_WIDTH)] + [_const_spec(c.shape) for c in consts],
        out_specs=row(d),
        out_shape=jax.ShapeDtypeStruct((n, d), F32),
        compiler_params=_params("parallel"),
        name="gated_merge",
    )(xn, u_pool, u_pool, z_ssm, y_nsa, *consts)


FF_CHUNK = 1024


def _ffn_kernel(x_ref, w1_ref, w2_ref, g_ref, b_ref, o_ref):
    x = x_ref[...]
    xb = x.astype(MXU_DTYPE)
    acc = jnp.zeros_like(x)
    for c in range(D_FF // FF_CHUNK):
        cols = slice(c * FF_CHUNK, (c + 1) * FF_CHUNK)
        h = jnp.maximum(_dot(xb, w1_ref[:, cols]), 0.0)
        acc = acc + _dot((h * h).astype(MXU_DTYPE), w2_ref[cols, :])
    o_ref[...] = _layer_norm(ALPHA * x + acc, g_ref[...], b_ref[...])


def _ffn(x1, w1, w2, g, b):
    n, d = x1.shape
    tm = ROW_TILE
    row = pl.BlockSpec((tm, d), lambda i: (i, 0))
    return pl.pallas_call(
        _ffn_kernel,
        grid=(n // tm,),
        in_specs=[row, _const_spec(w1.shape), _const_spec(w2.shape), _const_spec((1, d)), _const_spec((1, d))],
        out_specs=row,
        out_shape=jax.ShapeDtypeStruct((n, d), F32),
        compiler_params=_params("parallel"),
        name="relu2_mlp",
    )(x1, w1, w2, g.reshape(1, d), b.reshape(1, d))


def _hybrid_layer(xn, tabs, B, L, p):
    w_cat, w_gate = _expand_in_weight(p["w_in"])
    (u_pool, u_ssm, q2, kse, kso, kwd, kc, vsd, vwd, vc, gn) = _in_projection(xn, w_cat, tabs, L)
    mats = _s5_matrices(p["ssm_lam_re"], p["ssm_lam_im"], p["ssm_log_dt"], p["ssm_b_re"], p["ssm_b_im"],
                        p["ssm_c_re"], p["ssm_c_im"], p["ssm_d"])
    z_ssm = _s5_mixer(u_ssm, mats, B, L)
    kcmp, vcmp = _compress(kc, vc, _compress_weights(p["cmp_pe_k"], p["cmp_wk1"], p["cmp_wk2"]),
                           _compress_weights(p["cmp_pe_v"], p["cmp_wv1"], p["cmp_wv2"]), B, L)
    y_nsa = _nsa_attention(q2, gn, kcmp, vcmp, kse, kso, vsd, kwd, vwd, B, L)
    cast = lambda a: a.astype(MXU_DTYPE)
    row = lambda a: a.reshape(1, -1).astype(F32)
    w = dict(w_gate=w_gate, w_pool=cast(p["w_pool"]), pool_scale=row(p["pool_scale"]), w_glu=cast(p["w_glu"]),
             b_glu=row(p["b_glu"]), w_up_pool=cast(p["w_up_pool"]), w_up_ssm=cast(p["w_up_ssm"]),
             w_up_nsa=cast(p["w_up_nsa"]), w_out=cast(p["w_out"]), ln1_g=row(p["ln1_g"]), ln1_b=row(p["ln1_b"]))
    x1 = _merge(xn, u_pool, z_ssm, y_nsa, w, L)
    return _ffn(x1, cast(p["w_ff1"]), cast(p["w_ff2"]), p["ln2_g"], p["ln2_b"])


_LAYER_PARAMS = ("w_in", "w_pool", "pool_scale", "ssm_lam_re", "ssm_lam_im", "ssm_log_dt", "ssm_b_re", "ssm_b_im",
                 "ssm_c_re", "ssm_c_im", "ssm_d", "w_glu", "b_glu", "cmp_pe_k", "cmp_pe_v", "cmp_wk1", "cmp_wk2",
                 "cmp_wv1", "cmp_wv2", "w_up_pool", "w_up_ssm", "w_up_nsa", "w_out", "ln1_g", "ln1_b", "w_ff1",
                 "w_ff2", "ln2_g", "ln2_b")


def kernel(x, ln_in_g, ln_in_b, w_in, w_pool, pool_scale, ssm_lam_re, ssm_lam_im, ssm_log_dt, ssm_b_re, ssm_b_im, ssm_c_re, ssm_c_im, ssm_d, w_glu, b_glu, cmp_pe_k, cmp_pe_v, cmp_wk1, cmp_wk2, cmp_wv1, cmp_wv2, w_up_pool, w_up_ssm, w_up_nsa, w_out, ln1_g, ln1_b, w_ff1, w_ff2, ln2_g, ln2_b):
    B, L, D = x.shape
    assert D == D_MODEL and L % ROW_TILE == 0
    stacked = dict(zip(_LAYER_PARAMS, (w_in, w_pool, pool_scale, ssm_lam_re, ssm_lam_im, ssm_log_dt, ssm_b_re,
                                       ssm_b_im, ssm_c_re, ssm_c_im, ssm_d, w_glu, b_glu, cmp_pe_k, cmp_pe_v,
                                       cmp_wk1, cmp_wk2, cmp_wv1, cmp_wv2, w_up_pool, w_up_ssm, w_up_nsa, w_out,
                                       ln1_g, ln1_b, w_ff1, w_ff2, ln2_g, ln2_b)))
    tabs = _rope_tables(L)
    xn = _entry_norm(x.reshape(B * L, D), ln_in_g, ln_in_b)
    for i in range(w_in.shape[0]):
        xn = _hybrid_layer(xn, tabs, B, L, {k: v[i] for k, v in stacked.items()})
    return xn.reshape(B, L, D)
```

```python
import functools
import math

import numpy as np
import jax
import jax.numpy as jnp
from jax import lax
from jax.experimental import pallas as pl
from jax.experimental.pallas import tpu as pltpu

D_MODEL = 1024
DEPTH = 2
POOL_WIDTH = D_MODEL // 2
POOL_GROUPS = 4
POOL_WINDOWS = (2, 4, 8, 16)
POOL_GROUP_CH = POOL_WIDTH // POOL_GROUPS
SSM_WIDTH = D_MODEL // 2
SSM_GROUP_CH = 16
SSM_GROUPS = SSM_WIDTH // SSM_GROUP_CH
SSM_STATE = 64
NSA_HEADS = 8
NSA_KV_HEADS = 2
NSA_GROUP = NSA_HEADS // NSA_KV_HEADS
HEAD_DIM = 64
NSA_WIDTH = NSA_HEADS * HEAD_DIM
KV_WIDTH = NSA_KV_HEADS * HEAD_DIM
CMP_BLOCK = 32
CMP_STRIDE = 16
CMP_HIDDEN = 2 * HEAD_DIM
SEL_BLOCK = 64
SEL_TOPK = 16
WINDOW = 512
ROPE_THETA = 500000.0
ROPE_DIMS = HEAD_DIM // 4
ROPE_HALF = ROPE_DIMS // 2
NEG = -1e30
FORCE_SCORE = 1e6
D_FF = 4 * D_MODEL
ALPHA = (2 * DEPTH) ** 0.25
LN_EPS = 1e-5

LANES = 128
SUBLANES = 8
VMEM_LIMIT = 56 * 1024 * 1024

MXU_DTYPE = jnp.bfloat16
F32 = jnp.float32

LOG2E = math.log2(math.e)
SEL_BIAS = -(2.0 ** 100)
SSM_CHUNK = 16
MAX_SEL_BLOCKS = 64

ROW_TILE = 512
Q_TILE = 128
SEL_KEY_TILE = 512


def _nt_dot(a, b):
    return lax.dot_general(a, b, (((1,), (1,)), ((), ())), preferred_element_type=F32)


def _dot(a, b):
    return jnp.dot(a, b, preferred_element_type=F32)


def _split_dot(a, b, nt=False):
    f = _nt_dot if nt else _dot
    hi = a.astype(MXU_DTYPE)
    if MXU_DTYPE == jnp.float32:
        return (f(b, hi) if nt else f(hi, b))
    lo = (a - hi.astype(F32)).astype(MXU_DTYPE)
    if nt:
        return f(b, hi) + f(b, lo)
    return f(hi, b) + f(lo, b)


def _gelu_tanh(x):
    return x * (0.5 * (1.0 + jnp.tanh(math.sqrt(2.0 / math.pi) * (x + 0.044715 * (x * x * x)))))


def _layer_norm(xf, g, b):
    mu = jnp.mean(xf, axis=-1, keepdims=True)
    xc = xf - mu
    var = jnp.mean(xc * xc, axis=-1, keepdims=True)
    return xc * lax.rsqrt(var + LN_EPS) * g + b


def _params(*sem):
    return pltpu.CompilerParams(dimension_semantics=sem, vmem_limit_bytes=VMEM_LIMIT)


def _const_spec(shape):
    nd = len(shape)
    return pl.BlockSpec(shape, lambda *_: (0,) * nd, pipeline_mode=pl.Buffered(1))


def _ln_kernel(x_ref, g_ref, b_ref, o_ref):
    o_ref[...] = _layer_norm(x_ref[...], g_ref[...], b_ref[...])


def _entry_norm(x2, g, b):
    n, d = x2.shape
    tm = ROW_TILE
    return pl.pallas_call(
        _ln_kernel,
        grid=(n // tm,),
        in_specs=[pl.BlockSpec((tm, d), lambda i: (i, 0)), _const_spec((1, d)), _const_spec((1, d))],
        out_specs=pl.BlockSpec((tm, d), lambda i: (i, 0)),
        out_shape=jax.ShapeDtypeStruct((n, d), F32),
        compiler_params=_params("parallel"),
        name="entry_norm",
    )(x2, g.reshape(1, d), b.reshape(1, d))


_SEC = {}
_off = 0
for _name, _w in (("pool", 512), ("ssm", 512), ("q", 512), ("ksd", 256), ("kwd", 256),
                  ("kc", 128), ("vsd", 256), ("vwd", 256), ("vc", 128), ("gn", 128)):
    _SEC[_name] = (_off, _off + _w)
    _off += _w
IN_COLS = _off


def _inproj_kernel(x_ref, w_ref, cs_ref, s1_ref, s2_ref, oh_ref,
                   up_ref, us_ref, q_ref, ksx_ref, kwd_ref, kc_ref, vsd_ref, vwd_ref, vc_ref, gn_ref):
    xb = x_ref[...].astype(MXU_DTYPE)

    def proj(name):
        lo, hi = _SEC[name]
        return _dot(xb, w_ref[:, lo:hi])

    cs, s1, s2 = cs_ref[...], s1_ref[...], s2_ref[...]

    def rope(a):
        return a * cs + pltpu.roll(a, LANES - ROPE_HALF, 1) * s1 + pltpu.roll(a, ROPE_HALF, 1) * s2

    def roped_tiles(name):
        acc = proj(name)
        return [rope(acc[:, j * LANES:(j + 1) * LANES]) for j in range(acc.shape[1] // LANES)]

    def store_tiles(ref, tiles):
        for j, t in enumerate(tiles):
            ref[:, j * LANES:(j + 1) * LANES] = t.astype(ref.dtype)

    up_ref[...] = proj("pool")
    us_ref[...] = proj("ssm").astype(us_ref.dtype)
    store_tiles(q_ref, roped_tiles("q"))
    ks = roped_tiles("ksd")
    oh = oh_ref[...]
    store_tiles(ksx_ref, [ks[0], oh, ks[1], oh])
    store_tiles(kwd_ref, roped_tiles("kwd"))
    store_tiles(kc_ref, roped_tiles("kc"))
    vsd_ref[...] = proj("vsd").astype(vsd_ref.dtype)
    vwd_ref[...] = proj("vwd").astype(vwd_ref.dtype)
    vc_ref[...] = proj("vc")
    gn_ref[...] = jax.nn.sigmoid(proj("gn"))


def _in_projection(xn, w_cat, tabs, L):
    n, d = xn.shape
    tm = ROW_TILE
    nl = L // tm
    row = lambda w: pl.BlockSpec((tm, w), lambda i: (i, 0))
    tab = pl.BlockSpec((tm, LANES), lambda i: (i % nl, 0))
    bf = MXU_DTYPE
    outs = [(512, F32), (512, bf), (512, bf), (512, bf), (256, bf), (128, F32), (256, bf), (256, bf), (128, F32),
            (128, F32)]
    return pl.pallas_call(
        _inproj_kernel,
        grid=(n // tm,),
        in_specs=[row(d), _const_spec((d, IN_COLS)), tab, tab, tab, tab],
        out_specs=[row(w) for w, _ in outs],
        out_shape=[jax.ShapeDtypeStruct((n, w), dt) for w, dt in outs],
        compiler_params=_params("parallel"),
        name="in_projection",
    )(xn, w_cat, *tabs)


def _rope_tables(L):
    pos = jnp.arange(L, dtype=F32)
    inv_freq = ROPE_THETA ** (-jnp.arange(0, ROPE_DIMS, 2, dtype=F32) / ROPE_DIMS)
    ang = pos[:, None] * inv_freq[None, :]
    cos, sin = jnp.cos(ang), jnp.sin(ang)
    ones = jnp.ones((L, HEAD_DIM - ROPE_DIMS), F32)
    zeros = jnp.zeros((L, HEAD_DIM - ROPE_DIMS), F32)
    z8 = jnp.zeros((L, ROPE_HALF), F32)
    c64 = jnp.concatenate([cos, cos, ones], axis=1)
    s1_64 = jnp.concatenate([-sin, z8, zeros], axis=1)
    s2_64 = jnp.concatenate([z8, sin, zeros], axis=1)
    dup = lambda a: jnp.concatenate([a, a], axis=1)
    onehot = (jnp.arange(L)[:, None] // SEL_BLOCK == jnp.arange(MAX_SEL_BLOCKS)[None, :]).astype(F32)
    oh = jnp.concatenate([onehot, jnp.zeros((L, LANES - MAX_SEL_BLOCKS), F32)], axis=1)
    return dup(c64), dup(s1_64), dup(s2_64), oh


def _expand_in_weight(w_in):
    o = np.cumsum((0, POOL_WIDTH, SSM_WIDTH, NSA_WIDTH))
    w_pool, w_ssm = w_in[:, o[0]:o[1]], w_in[:, o[1]:o[2]]
    w_q = w_in[:, o[2]:o[3]] * (LOG2E / math.sqrt(HEAD_DIM))
    kv0 = int(o[3])
    kvs = [w_in[:, kv0 + i * KV_WIDTH: kv0 + (i + 1) * KV_WIDTH] for i in range(6)]
    kc, vc, ks, vs, kw, vw = kvs
    head = lambda a, k: a[:, k * HEAD_DIM:(k + 1) * HEAD_DIM]
    dupl = lambda a: jnp.concatenate([head(a, 0), head(a, 0), head(a, 1), head(a, 1)], axis=1)
    g0 = kv0 + 6 * KV_WIDTH
    gn = w_in[:, g0:g0 + 3 * NSA_HEADS]
    gn = jnp.pad(gn, ((0, 0), (0, LANES - 3 * NSA_HEADS)))
    w_cat = jnp.concatenate([w_pool, w_ssm, w_q, dupl(ks), dupl(kw), kc, dupl(vs), dupl(vw), vc, gn], axis=1)
    w_gate = w_in[:, g0 + 3 * NSA_HEADS:]
    return w_cat.astype(MXU_DTYPE), w_gate.astype(MXU_DTYPE)


def _s5_matrices(lam_re, lam_im, log_dt, b_re, b_im, c_re, c_im, d_skip):
    T = SSM_CHUNK
    hp = lax.Precision.HIGHEST
    step = jnp.exp(log_dt)[None, :, None]
    n = jnp.arange(T + 1, dtype=F32)[:, None, None]
    mag = jnp.exp(n * lam_re[None] * step)
    ang = n * lam_im[None] * step
    pr, pi = mag * jnp.cos(ang), mag * jnp.sin(ang)
    den = lam_re * lam_re + lam_im * lam_im
    n_re, n_im = pr[1] - 1.0, pi[1]
    k_re = (n_re * lam_re + n_im * lam_im) / den
    k_im = (n_im * lam_re - n_re * lam_im) / den
    bb_re = k_re[..., None] * b_re - k_im[..., None] * b_im
    bb_im = k_re[..., None] * b_im + k_im[..., None] * b_re
    ca_re = c_re[None] * pr[:, :, None, :] - c_im[None] * pi[:, :, None, :]
    ca_im = c_re[None] * pi[:, :, None, :] + c_im[None] * pr[:, :, None, :]
    taps = (jnp.einsum('tgcp,gpd->tgcd', ca_re[:T], bb_re, precision=hp)
            - jnp.einsum('tgcp,gpd->tgcd', ca_im[:T], bb_im, precision=hp))
    eye = jnp.eye(SSM_GROUP_CH, dtype=F32)
    taps = taps.at[0].add(d_skip[:, :, None] * eye[None])
    s_idx = jnp.arange(T)[:, None]
    t_idx = jnp.arange(T)[None, :]
    lag = t_idx - s_idx
    toe = taps[jnp.clip(lag, 0, T - 1)] * (lag >= 0)[:, :, None, None, None].astype(F32)
    w_toe = toe.transpose(2, 0, 4, 1, 3).reshape(SSM_GROUPS, T * SSM_GROUP_CH, T * SSM_GROUP_CH)
    rev_r, rev_i = pr[:T][::-1], pi[:T][::-1]
    bc_re = rev_r[..., None] * bb_re[None] - rev_i[..., None] * bb_im[None]
    bc_im = rev_r[..., None] * bb_im[None] + rev_i[..., None] * bb_re[None]
    to_in = lambda a: a.transpose(1, 0, 3, 2).reshape(SSM_GROUPS, T * SSM_GROUP_CH, SSM_STATE)
    to_out = lambda a: a.transpose(1, 3, 0, 2).reshape(SSM_GROUPS, SSM_STATE, T * SSM_GROUP_CH)
    cc_re, cc_im = to_out(ca_re[1:]), to_out(-ca_im[1:])
    a_t = (pr[T].reshape(SSM_GROUPS, 1, SSM_STATE), pi[T].reshape(SSM_GROUPS, 1, SSM_STATE))
    cast = lambda a: a.astype(MXU_DTYPE)
    return cast(w_toe), cast(to_in(bc_re)), cast(to_in(bc_im)), cast(cc_re), cast(cc_im), a_t[0], a_t[1]


def _s5_kernel(u_ref, wt_ref, bcr_ref, bci_ref, ccr_ref, cci_ref, ar_ref, ai_ref, z_ref,
               sr_ref, si_ref, hr_ref, hi_ref, *, nchunk, bp):
    u = u_ref[...]
    sr_ref[...] = _dot(u, bcr_ref[...])
    si_ref[...] = _dot(u, bci_ref[...])
    ar = jnp.broadcast_to(ar_ref[...], (bp, SSM_STATE))
    ai = jnp.broadcast_to(ai_ref[...], (bp, SSM_STATE))

    def body(c, carry):
        hr, hi = carry
        rows = pl.ds(pl.multiple_of(c * bp, bp), bp)
        hr_ref[rows, :] = hr
        hi_ref[rows, :] = hi
        return (ar * hr - ai * hi + sr_ref[rows, :], ar * hi + ai * hr + si_ref[rows, :])

    zero = jnp.zeros((bp, SSM_STATE), F32)
    lax.fori_loop(0, nchunk, body, (zero, zero))
    y = (_dot(u, wt_ref[...])
         + _dot(hr_ref[...].astype(MXU_DTYPE), ccr_ref[...])
         + _dot(hi_ref[...].astype(MXU_DTYPE), cci_ref[...]))
    z_ref[...] = _gelu_tanh(y).astype(z_ref.dtype)


def _s5_mixer(u_ssm, mats, B, L):
    T, G, C = SSM_CHUNK, SSM_GROUPS, SSM_GROUP_CH
    nchunk = L // T
    bp = -(-B // SUBLANES) * SUBLANES
    u5 = u_ssm.reshape(B, nchunk, T, G, C)
    if bp != B:
        u5 = jnp.pad(u5, ((0, bp - B), (0, 0), (0, 0), (0, 0), (0, 0)))
    u_r = u5.transpose(3, 1, 0, 2, 4).reshape(G, nchunk * bp, T * C)
    rows, tc = nchunk * bp, T * C
    grp = lambda *s: pl.BlockSpec((None,) + s, lambda g: (g,) + (0,) * len(s))
    z_r = pl.pallas_call(
        functools.partial(_s5_kernel, nchunk=nchunk, bp=bp),
        grid=(G,),
        in_specs=[grp(rows, tc), grp(tc, tc), grp(tc, SSM_STATE), grp(tc, SSM_STATE),
                  grp(SSM_STATE, tc), grp(SSM_STATE, tc), grp(1, SSM_STATE), grp(1, SSM_STATE)],
        out_specs=grp(rows, tc),
        out_shape=jax.ShapeDtypeStruct((G, rows, tc), MXU_DTYPE),
        scratch_shapes=[pltpu.VMEM((rows, SSM_STATE), F32) for _ in range(4)],
        compiler_params=_params("parallel"),
        name="s5_mixer",
    )(u_r, *mats)
    z5 = z_r.reshape(G, nchunk, bp, T, C)[:, :, :B]
    return z5.transpose(2, 1, 3, 0, 4).reshape(B * L, G * C)


def _compress_kernel(kc_ref, vc_ref, pek_ref, pev_ref, wk1_ref, wk2_ref, wv1_ref, wv2_ref, ko_ref, vo_ref, *, nrow):
    def one(src_ref, pe_ref, w1_ref, w2_ref, out_ref):
        first = jnp.zeros((nrow, NSA_KV_HEADS * CMP_HIDDEN), F32)
        second = jnp.zeros((nrow, NSA_KV_HEADS * CMP_HIDDEN), F32)
        for t in range(CMP_STRIDE):
            x = src_ref[pl.ds(t, nrow, stride=CMP_STRIDE), :]
            first = first + _dot((x + pe_ref[t:t + 1, :]).astype(MXU_DTYPE), w1_ref[t])
            u = CMP_STRIDE + t
            second = second + _dot((x + pe_ref[u:u + 1, :]).astype(MXU_DTYPE), w1_ref[u])
        hid = first + pltpu.roll(second, nrow - 1, 0)
        act = _gelu_tanh(hid).astype(MXU_DTYPE)
        for k in range(NSA_KV_HEADS):
            out_ref[k] = _dot(act, w2_ref[k]).astype(out_ref.dtype)

    one(kc_ref, pek_ref, wk1_ref, wk2_ref, ko_ref)
    one(vc_ref, pev_ref, wv1_ref, wv2_ref, vo_ref)


def _compress_weights(pe, w1, w2):
    w1r = w1.reshape(CMP_BLOCK, HEAD_DIM, CMP_HIDDEN)
    eye = jnp.eye(NSA_KV_HEADS, dtype=w1.dtype)
    w1x = jnp.einsum('tdj,hg->thdgj', w1r, eye).reshape(CMP_BLOCK, KV_WIDTH, NSA_KV_HEADS * CMP_HIDDEN)
    pe2 = jnp.concatenate([pe] * NSA_KV_HEADS, axis=1)
    w2k = []
    for k in range(NSA_KV_HEADS):
        blk = jnp.zeros((NSA_KV_HEADS, CMP_HIDDEN, 2, HEAD_DIM), w2.dtype)
        blk = blk.at[k].set(jnp.broadcast_to(w2[:, None, :], (CMP_HIDDEN, 2, HEAD_DIM)))
        w2k.append(blk.reshape(NSA_KV_HEADS * CMP_HIDDEN, 2 * HEAD_DIM))
    return pe2.astype(F32), w1x.astype(MXU_DTYPE), jnp.stack(w2k).astype(MXU_DTYPE)


def _compress(kc, vc, wk, wv, B, L):
    nrow = L // CMP_STRIDE
    src = pl.BlockSpec((L, KV_WIDTH), lambda b: (b, 0))
    out = pl.BlockSpec((None, NSA_KV_HEADS, nrow, LANES), lambda b: (b, 0, 0, 0))
    pek, wk1, wk2 = wk
    pev, wv1, wv2 = wv
    consts = [pek, pev, wk1, wk2, wv1, wv2]
    shape = jax.ShapeDtypeStruct((B, NSA_KV_HEADS, nrow, LANES), MXU_DTYPE)
    return pl.pallas_call(
        functools.partial(_compress_kernel, nrow=nrow),
        grid=(B,),
        in_specs=[src, src] + [_const_spec(c.shape) for c in consts],
        out_specs=[out, out],
        out_shape=[shape, shape],
        compiler_params=_params("parallel"),
        name="kv_compress",
    )(kc, vc, *consts)


def _nsa_kernel(q_ref, gn_ref, kcmp_ref, vcmp_ref, ksx_ref, vsd_ref, kwd_ref, vwd_ref, ovt_ref, gexp_ref,
                o_ref, *, tq, tkb, ncmp, topk):
    q0 = pl.multiple_of(pl.program_id(1) * tq, tq)
    lo = lax.broadcasted_iota(jnp.int32, (tq, LANES), 1) < HEAD_DIM
    qt = [q_ref[:, j * LANES:(j + 1) * LANES] for j in range(NSA_HEADS // 2)]
    qzero = jnp.zeros_like(qt[0])
    qm = [jnp.where(lo, qt[h // 2], qzero) if h % 2 == 0 else jnp.where(lo, qzero, qt[h // 2])
          for h in range(NSA_HEADS)]
    group_rows = lambda k, parts: jnp.concatenate(parts[NSA_GROUP * k: NSA_GROUP * (k + 1)], axis=0)

    t_c = q0 + lax.broadcasted_iota(jnp.int32, (tq, ncmp), 0)
    n_c = lax.broadcasted_iota(jnp.int32, (tq, ncmp), 1)
    cvalid = n_c * CMP_STRIDE + (CMP_BLOCK - 1) <= t_c
    o_cmp = [None] * NSA_HEADS
    bias = [None] * NSA_KV_HEADS
    j_s = lax.broadcasted_iota(jnp.int32, (MAX_SEL_BLOCKS, tq), 0)
    cur = (q0 + lax.broadcasted_iota(jnp.int32, (MAX_SEL_BLOCKS, tq), 1)) // SEL_BLOCK
    forced = (j_s == 0) | (j_s == cur) | (j_s == cur - 1)
    for k in range(NSA_KV_HEADS):
        kcd, vcd = kcmp_ref[k], vcmp_ref[k]
        psum = jnp.zeros((tq, ncmp), F32)
        for g in range(NSA_GROUP):
            h = NSA_GROUP * k + g
            sm = jnp.where(cvalid, _nt_dot(qm[h], kcd), NEG)
            m = jnp.max(sm, axis=-1, keepdims=True)
            e = jnp.where(cvalid, jnp.exp2(sm - m), 0.0)
            den = jnp.sum(e, axis=-1, keepdims=True)
            p = e / jnp.where(den > 0.0, den, 1.0)
            o_cmp[h] = _dot(p.astype(MXU_DTYPE), vcd)
            psum = psum + p
        imp_t = _split_dot(psum, ovt_ref[...], nt=True)
        score = jnp.where(forced, FORCE_SCORE, jnp.where(j_s <= cur, imp_t, -FORCE_SCORE))
        nblk = MAX_SEL_BLOCKS // SUBLANES
        sblk = [score[r * SUBLANES:(r + 1) * SUBLANES] for r in range(nblk)]
        jblk = j_s[0:SUBLANES]
        cnt = [jnp.zeros((SUBLANES, tq), F32) for _ in range(nblk)]
        for i in range(MAX_SEL_BLOCKS):
            ri = jnp.broadcast_to(score[i:i + 1, :], (SUBLANES, tq))
            for r in range(nblk):
                if r * SUBLANES > i:
                    beats = ri >= sblk[r]
                elif r * SUBLANES + SUBLANES - 1 < i:
                    beats = ri > sblk[r]
                else:
                    tie = jnp.where(jblk + r * SUBLANES > i, 1.0, 0.0)
                    beats = jnp.where(ri > sblk[r], 1.0, jnp.where(ri == sblk[r], tie, 0.0)) > 0.5
                cnt[r] = cnt[r] + jnp.where(beats, 1.0, 0.0)
        drop = jnp.concatenate([jnp.where(c < float(topk), 0.0, 1.0) for c in cnt] * 2, axis=0)
        bias[k] = (jnp.transpose(drop) * SEL_BIAS).astype(MXU_DTYPE)

    rows4 = NSA_GROUP * tq
    t_d = q0 + lax.broadcasted_iota(jnp.int32, (tq, tkb), 0)
    t_d = jnp.concatenate([t_d] * NSA_GROUP, axis=0)
    col_d = lax.broadcasted_iota(jnp.int32, (rows4, tkb), 1)
    nfull = q0 // tkb
    q_sel = [group_rows(k, [jnp.concatenate([qm[h], bias[h // NSA_GROUP]], axis=1) for h in range(NSA_HEADS)])
             for k in range(NSA_KV_HEADS)]

    def sel_step(kb, carry, diagonal):
        rows = pl.ds(pl.multiple_of(kb * tkb, tkb), tkb)
        out = []
        for k in range(NSA_KV_HEADS):
            m, l, acc = carry[k]
            s = _nt_dot(q_sel[k], ksx_ref[rows, 2 * k * LANES:(2 * k + 2) * LANES])
            if diagonal:
                s = jnp.where(col_d + kb * tkb <= t_d, s, NEG)
            m_new = jnp.maximum(m, jnp.max(s, axis=-1, keepdims=True))
            alpha = jnp.exp2(m - m_new)
            p = jnp.exp2(s - m_new)
            l = alpha * l + jnp.sum(p, axis=-1, keepdims=True)
            acc = alpha * acc + _dot(p.astype(MXU_DTYPE), vsd_ref[rows, k * LANES:(k + 1) * LANES])
            out.append((m_new, l, acc))
        return tuple(out)

    init = tuple((jnp.full((rows4, 1), NEG, F32), jnp.zeros((rows4, 1), F32), jnp.zeros((rows4, LANES), F32))
                 for _ in range(NSA_KV_HEADS))
    carry = lax.fori_loop(0, nfull, lambda kb, c: sel_step(kb, c, False), init)
    o_sel = [acc / l for (_, l, acc) in sel_step(nfull, carry, True)]

    span = WINDOW + tq
    start = pl.multiple_of(jnp.maximum(q0 - WINDOW, 0), tq)
    t_w = q0 + lax.broadcasted_iota(jnp.int32, (tq, span), 0)
    k_w = start + lax.broadcasted_iota(jnp.int32, (tq, span), 1)
    wvalid = (k_w <= t_w) & (t_w - k_w < WINDOW)
    wvalid = jnp.concatenate([wvalid] * NSA_GROUP, axis=0)
    o_win = []
    for k in range(NSA_KV_HEADS):
        lanes = slice(k * LANES, (k + 1) * LANES)
        s = _nt_dot(group_rows(k, qm), kwd_ref[pl.ds(start, span), lanes])
        sm = jnp.where(wvalid, s, NEG)
        e = jnp.exp2(sm - jnp.max(sm, axis=-1, keepdims=True))
        den = jnp.sum(e, axis=-1, keepdims=True)
        o_win.append(_dot(e.astype(MXU_DTYPE), vwd_ref[pl.ds(start, span), lanes]) / den)

    gates = _split_dot(gn_ref[...], gexp_ref[...])
    for j in range(NSA_HEADS // 2):
        k = (2 * j) // NSA_GROUP
        r_e = slice(((2 * j) % NSA_GROUP) * tq, ((2 * j) % NSA_GROUP + 1) * tq)
        r_o = slice(((2 * j + 1) % NSA_GROUP) * tq, ((2 * j + 1) % NSA_GROUP + 1) * tq)
        cmp_t = jnp.where(lo, o_cmp[2 * j], o_cmp[2 * j + 1])
        sel_t = jnp.where(lo, o_sel[k][r_e], o_sel[k][r_o])
        win_t = jnp.where(lo, o_win[k][r_e], o_win[k][r_o])
        gl = lambda br: gates[:, br * NSA_WIDTH + j * LANES: br * NSA_WIDTH + (j + 1) * LANES]
        o_ref[:, j * LANES:(j + 1) * LANES] = (gl(0) * cmp_t + gl(1) * sel_t + gl(2) * win_t).astype(o_ref.dtype)


def _overlap_t(ncmp, n_sel):
    cs = np.arange(ncmp)[None, :] * CMP_STRIDE
    ss = np.arange(MAX_SEL_BLOCKS)[:, None] * SEL_BLOCK
    ov = np.minimum(cs + CMP_BLOCK, ss + SEL_BLOCK) - np.maximum(cs, ss)
    ov = np.maximum(ov, 0) / CMP_STRIDE
    ov[n_sel:] = 0.0
    ov[:, (ncmp - 1):] = 0.0
    return jnp.asarray(ov, dtype=MXU_DTYPE)


def _gate_expand():
    e = np.zeros((LANES, 3 * NSA_WIDTH), np.float32)
    for h in range(NSA_HEADS):
        for br in range(3):
            e[h * 3 + br, br * NSA_WIDTH + h * HEAD_DIM: br * NSA_WIDTH + (h + 1) * HEAD_DIM] = 1.0
    return jnp.asarray(e, dtype=MXU_DTYPE)


def _nsa_attention(q2, gn, kcmp, vcmp, ksx, vsd, kwd, vwd, B, L):
    tq = Q_TILE
    tkb = min(SEL_KEY_TILE, L)
    ncmp = L // CMP_STRIDE
    n_sel = L // SEL_BLOCK
    assert n_sel <= MAX_SEL_BLOCKS and L >= WINDOW + tq and L % tkb == 0 and tkb % tq == 0
    nq = L // tq
    seq = lambda a: a.reshape(B, L, a.shape[-1])
    kv_spec = lambda w: pl.BlockSpec((None, L, w), lambda b, i: (b, 0, 0))
    cmp_spec = pl.BlockSpec((None, NSA_KV_HEADS, ncmp, LANES), lambda b, i: (b, 0, 0, 0))
    row = lambda w: pl.BlockSpec((tq, w), lambda b, i: (b * nq + i, 0))
    ovt, gexp = _overlap_t(ncmp, n_sel), _gate_expand()
    return pl.pallas_call(
        functools.partial(_nsa_kernel, tq=tq, tkb=tkb, ncmp=ncmp, topk=min(SEL_TOPK, n_sel)),
        grid=(B, nq),
        in_specs=[row(NSA_WIDTH), row(LANES), cmp_spec, cmp_spec, kv_spec(4 * LANES), kv_spec(2 * LANES),
                  kv_spec(2 * LANES), kv_spec(2 * LANES), _const_spec(ovt.shape), _const_spec(gexp.shape)],
        out_specs=row(NSA_WIDTH),
        out_shape=jax.ShapeDtypeStruct((B * L, NSA_WIDTH), MXU_DTYPE),
        compiler_params=_params("parallel", "arbitrary"),
        name="nsa_attention",
    )(q2, gn, kcmp, vcmp, seq(ksx), seq(vsd), seq(kwd), seq(vwd), ovt, gexp)


POOL_HALO = 16


def _merge_kernel(x_ref, up_ref, halo_ref, zs_ref, yn_ref, wg_ref, wpool_ref, pscale_ref, wglu_ref, bglu_ref,
                  wup_p_ref, wup_s_ref, wup_n_ref, wout_ref, g_ref, b_ref, o_ref, *, tm, nl):
    x = x_ref[...]
    xb = x.astype(MXU_DTYPE)
    first_tile = (pl.program_id(0) % nl) == 0
    halo = jnp.where(first_tile, 0.0, halo_ref[...])
    ext = jnp.concatenate([halo, up_ref[...]], axis=0)
    t_in_seq = (pl.program_id(0) % nl) * tm + lax.broadcasted_iota(jnp.int32, (tm, 1), 0)
    y_pool = []
    for gi, w in enumerate(POOL_WINDOWS):
        lanes = slice(gi * POOL_GROUP_CH, (gi + 1) * POOL_GROUP_CH)
        e = ext[:, lanes]
        span = 1
        while span < w:
            e = e + pltpu.roll(e, span, 0)
            span *= 2
        cnt = jnp.minimum(t_in_seq + 1, w).astype(F32)
        z = e[POOL_HALO:] / cnt - ext[POOL_HALO:, lanes]
        y_pool.append(_dot(z.astype(MXU_DTYPE), wpool_ref[gi]))
    y_pool = jnp.concatenate(y_pool, axis=1) * pscale_ref[...]
    zs = zs_ref[...]
    y_ssm = zs.astype(F32) * jax.nn.sigmoid(_dot(zs, wglu_ref[...]) + bglu_ref[...])
    d = x.shape[1]
    merged = jnp.zeros_like(x)
    for br, (y, w_ref) in enumerate(((y_pool.astype(MXU_DTYPE), wup_p_ref), (y_ssm.astype(MXU_DTYPE), wup_s_ref),
                                     (yn_ref[...], wup_n_ref))):
        gate = jax.nn.sigmoid(_dot(xb, wg_ref[:, br * d:(br + 1) * d]))
        merged = merged + gate * _dot(y, w_ref[...])
    o_ref[...] = _layer_norm(ALPHA * x + _dot(merged.astype(MXU_DTYPE), wout_ref[...]), g_ref[...], b_ref[...])


def _merge(xn, u_pool, z_ssm, y_nsa, w, L):
    n, d = xn.shape
    tm = ROW_TILE // 2
    nl = L // tm
    row = lambda wd: pl.BlockSpec((tm, wd), lambda i: (i, 0))
    halo = pl.BlockSpec((POOL_HALO, POOL_WIDTH), lambda i: (jnp.maximum(i * (tm // POOL_HALO) - 1, 0), 0))
    consts = [w["w_gate"], w["w_pool"], w["pool_scale"], w["w_glu"], w["b_glu"], w["w_up_pool"], w["w_up_ssm"],
              w["w_up_nsa"], w["w_out"], w["ln1_g"], w["ln1_b"]]
    return pl.pallas_call(
        functools.partial(_merge_kernel, tm=tm, nl=nl),
        grid=(n // tm,),
        in_specs=[row(d), row(POOL_WIDTH), halo, row(SSM_WIDTH), row(NSA_WIDTH)] + [_const_spec(c.shape) for c in consts],
        out_specs=row(d),
        out_shape=jax.ShapeDtypeStruct((n, d), F32),
        compiler_params=_params("parallel"),
        name="gated_merge",
    )(xn, u_pool, u_pool, z_ssm, y_nsa, *consts)


FF_CHUNK = 1024


def _ffn_kernel(x_ref, w1_ref, w2_ref, g_ref, b_ref, o_ref):
    x = x_ref[...]
    xb = x.astype(MXU_DTYPE)
    acc = jnp.zeros_like(x)
    for c in range(D_FF // FF_CHUNK):
        cols = slice(c * FF_CHUNK, (c + 1) * FF_CHUNK)
        h = jnp.maximum(_dot(xb, w1_ref[:, cols]), 0.0)
        acc = acc + _dot((h * h).astype(MXU_DTYPE), w2_ref[cols, :])
    o_ref[...] = _layer_norm(ALPHA * x + acc, g_ref[...], b_ref[...])


def _ffn(x1, w1, w2, g, b):
    n, d = x1.shape
    tm = ROW_TILE
    row = pl.BlockSpec((tm, d), lambda i: (i, 0))
    return pl.pallas_call(
        _ffn_kernel,
        grid=(n // tm,),
        in_specs=[row, _const_spec(w1.shape), _const_spec(w2.shape), _const_spec((1, d)), _const_spec((1, d))],
        out_specs=row,
        out_shape=jax.ShapeDtypeStruct((n, d), F32),
        compiler_params=_params("parallel"),
        name="relu2_mlp",
    )(x1, w1, w2, g.reshape(1, d), b.reshape(1, d))


def _hybrid_layer(xn, tabs, B, L, p):
    w_cat, w_gate = _expand_in_weight(p["w_in"])
    (u_pool, u_ssm, q2, ksx, kwd, kc, vsd, vwd, vc, gn) = _in_projection(xn, w_cat, tabs, L)
    mats = _s5_matrices(p["ssm_lam_re"], p["ssm_lam_im"], p["ssm_log_dt"], p["ssm_b_re"], p["ssm_b_im"],
                        p["ssm_c_re"], p["ssm_c_im"], p["ssm_d"])
    z_ssm = _s5_mixer(u_ssm, mats, B, L)
    kcmp, vcmp = _compress(kc, vc, _compress_weights(p["cmp_pe_k"], p["cmp_wk1"], p["cmp_wk2"]),
                           _compress_weights(p["cmp_pe_v"], p["cmp_wv1"], p["cmp_wv2"]), B, L)
    y_nsa = _nsa_attention(q2, gn, kcmp, vcmp, ksx, vsd, kwd, vwd, B, L)
    cast = lambda a: a.astype(MXU_DTYPE)
    row = lambda a: a.reshape(1, -1).astype(F32)
    w = dict(w_gate=w_gate, w_pool=cast(p["w_pool"]), pool_scale=row(p["pool_scale"]), w_glu=cast(p["w_glu"]),
             b_glu=row(p["b_glu"]), w_up_pool=cast(p["w_up_pool"]), w_up_ssm=cast(p["w_up_ssm"]),
             w_up_nsa=cast(p["w_up_nsa"]), w_out=cast(p["w_out"]), ln1_g=row(p["ln1_g"]), ln1_b=row(p["ln1_b"]))
    x1 = _merge(xn, u_pool, z_ssm, y_nsa, w, L)
    return _ffn(x1, cast(p["w_ff1"]), cast(p["w_ff2"]), p["ln2_g"], p["ln2_b"])


_LAYER_PARAMS = ("w_in", "w_pool", "pool_scale", "ssm_lam_re", "ssm_lam_im", "ssm_log_dt", "ssm_b_re", "ssm_b_im",
                 "ssm_c_re", "ssm_c_im", "ssm_d", "w_glu", "b_glu", "cmp_pe_k", "cmp_pe_v", "cmp_wk1", "cmp_wk2",
                 "cmp_wv1", "cmp_wv2", "w_up_pool", "w_up_ssm", "w_up_nsa", "w_out", "ln1_g", "ln1_b", "w_ff1",
                 "w_ff2", "ln2_g", "ln2_b")


def kernel(x, ln_in_g, ln_in_b, w_in, w_pool, pool_scale, ssm_lam_re, ssm_lam_im, ssm_log_dt, ssm_b_re, ssm_b_im, ssm_c_re, ssm_c_im, ssm_d, w_glu, b_glu, cmp_pe_k, cmp_pe_v, cmp_wk1, cmp_wk2, cmp_wv1, cmp_wv2, w_up_pool, w_up_ssm, w_up_nsa, w_out, ln1_g, ln1_b, w_ff1, w_ff2, ln2_g, ln2_b):
    B, L, D = x.shape
    assert D == D_MODEL and L % ROW_TILE == 0
    stacked = dict(zip(_LAYER_PARAMS, (w_in, w_pool, pool_scale, ssm_lam_re, ssm_lam_im, ssm_log_dt, ssm_b_re,
                                       ssm_b_im, ssm_c_re, ssm_c_im, ssm_d, w_glu, b_glu, cmp_pe_k, cmp_pe_v,
                                       cmp_wk1, cmp_wk2, cmp_wv1, cmp_wv2, w_up_pool, w_up_ssm, w_up_nsa, w_out,
                                       ln1_g, ln1_b, w_ff1, w_ff2, ln2_g, ln2_b)))
    tabs = _rope_tables(L)
    xn = _entry_norm(x.reshape(B * L, D), ln_in_g, ln_in_b)
    for i in range(w_in.shape[0]):
        xn = _hybrid_layer(xn, tabs, B, L, {k: v[i] for k, v in stacked.items()})
    return xn.reshape(B, L, D)
```

```python
import functools
import math

import numpy as np
import jax
import jax.numpy as jnp
from jax import lax
from jax.experimental import pallas as pl
from jax.experimental.pallas import tpu as pltpu

D_MODEL = 1024
DEPTH = 2
POOL_WIDTH = D_MODEL // 2
POOL_GROUPS = 4
POOL_WINDOWS = (2, 4, 8, 16)
POOL_GROUP_CH = POOL_WIDTH // POOL_GROUPS
SSM_WIDTH = D_MODEL // 2
SSM_GROUP_CH = 16
SSM_GROUPS = SSM_WIDTH // SSM_GROUP_CH
SSM_STATE = 64
NSA_HEADS = 8
NSA_KV_HEADS = 2
NSA_GROUP = NSA_HEADS // NSA_KV_HEADS
HEAD_DIM = 64
NSA_WIDTH = NSA_HEADS * HEAD_DIM
KV_WIDTH = NSA_KV_HEADS * HEAD_DIM
CMP_BLOCK = 32
CMP_STRIDE = 16
CMP_HIDDEN = 2 * HEAD_DIM
SEL_BLOCK = 64
SEL_TOPK = 16
WINDOW = 512
ROPE_THETA = 500000.0
ROPE_DIMS = HEAD_DIM // 4
ROPE_HALF = ROPE_DIMS // 2
NEG = -1e30
FORCE_SCORE = 1e6
D_FF = 4 * D_MODEL
ALPHA = (2 * DEPTH) ** 0.25
LN_EPS = 1e-5

LANES = 128
SUBLANES = 8
VMEM_LIMIT = 56 * 1024 * 1024

MXU_DTYPE = jnp.bfloat16
F32 = jnp.float32

LOG2E = math.log2(math.e)
SEL_BIAS = -(2.0 ** 100)
SSM_CHUNK = 8
MAX_SEL_BLOCKS = 64

ROW_TILE = 512
Q_TILE = 128
SEL_KEY_TILE = 512


def _nt_dot(a, b):
    return lax.dot_general(a, b, (((1,), (1,)), ((), ())), preferred_element_type=F32)


def _dot(a, b):
    return jnp.dot(a, b, preferred_element_type=F32)


def _gelu_tanh(x):
    return x * (0.5 * (1.0 + jnp.tanh(math.sqrt(2.0 / math.pi) * (x + 0.044715 * (x * x * x)))))


def _layer_norm(xf, g, b):
    mu = jnp.mean(xf, axis=-1, keepdims=True)
    xc = xf - mu
    var = jnp.mean(xc * xc, axis=-1, keepdims=True)
    return xc * lax.rsqrt(var + LN_EPS) * g + b


def _params(*sem):
    return pltpu.CompilerParams(dimension_semantics=sem, vmem_limit_bytes=VMEM_LIMIT)


def _const_spec(shape):
    nd = len(shape)
    return pl.BlockSpec(shape, lambda *_: (0,) * nd, pipeline_mode=pl.Buffered(1))


def _ln_kernel(x_ref, g_ref, b_ref, o_ref):
    o_ref[...] = _layer_norm(x_ref[...], g_ref[...], b_ref[...])


def _entry_norm(x2, g, b):
    n, d = x2.shape
    tm = ROW_TILE
    return pl.pallas_call(
        _ln_kernel,
        grid=(n // tm,),
        in_specs=[pl.BlockSpec((tm, d), lambda i: (i, 0)), _const_spec((1, d)), _const_spec((1, d))],
        out_specs=pl.BlockSpec((tm, d), lambda i: (i, 0)),
        out_shape=jax.ShapeDtypeStruct((n, d), F32),
        compiler_params=_params("parallel"),
        name="entry_norm",
    )(x2, g.reshape(1, d), b.reshape(1, d))


_SEC = {}
_off = 0
for _name, _w in (("pool", 512), ("ssm", 512), ("q", 512), ("ksd", 256), ("kwd", 256),
                  ("kc", 128), ("vc", 128), ("gn", 128)):
    _SEC[_name] = (_off, _off + _w)
    _off += _w
IN_COLS = _off


def _inproj_kernel(x_ref, w_ref, wvt_ref, cs_ref, s1_ref, s2_ref, oh_ref,
                   up_ref, us_ref, q_ref, ksx_ref, kwd_ref, kc_ref, vst_ref, vwt_ref, vc_ref, gn_ref):
    xb = x_ref[...].astype(MXU_DTYPE)

    def proj(name):
        lo, hi = _SEC[name]
        return _dot(xb, w_ref[:, lo:hi])

    cs, s1, s2 = cs_ref[...], s1_ref[...], s2_ref[...]

    def rope(a):
        return a * cs + pltpu.roll(a, LANES - ROPE_HALF, 1) * s1 + pltpu.roll(a, ROPE_HALF, 1) * s2

    def roped_tiles(name):
        acc = proj(name)
        return [rope(acc[:, j * LANES:(j + 1) * LANES]) for j in range(acc.shape[1] // LANES)]

    def store_tiles(ref, tiles):
        for j, t in enumerate(tiles):
            ref[:, j * LANES:(j + 1) * LANES] = t.astype(ref.dtype)

    up_ref[...] = proj("pool")
    us_ref[...] = proj("ssm")
    store_tiles(q_ref, roped_tiles("q"))
    ks = roped_tiles("ksd")
    oh = oh_ref[...]
    store_tiles(ksx_ref, [ks[0], oh, ks[1], oh])
    store_tiles(kwd_ref, roped_tiles("kwd"))
    store_tiles(kc_ref, roped_tiles("kc"))
    vt = _nt_dot(wvt_ref[...], xb)
    vst_ref[...] = vt[0:KV_WIDTH].astype(vst_ref.dtype)
    vwt_ref[...] = vt[KV_WIDTH:2 * KV_WIDTH].astype(vwt_ref.dtype)
    vc_ref[...] = proj("vc")
    gn_ref[...] = jax.nn.sigmoid(proj("gn"))


def _in_projection(xn, w_cat, w_vt, tabs, B, L):
    n, d = xn.shape
    tm = ROW_TILE
    nl = L // tm
    row = lambda w: pl.BlockSpec((tm, w), lambda i: (i, 0))
    tab = pl.BlockSpec((tm, LANES), lambda i: (i % nl, 0))
    tr = pl.BlockSpec((None, KV_WIDTH, tm), lambda i: (i // nl, 0, i % nl))
    bf = MXU_DTYPE
    rows = lambda w, dt: (row(w), jax.ShapeDtypeStruct((n, w), dt))
    trs = (tr, jax.ShapeDtypeStruct((B, KV_WIDTH, L), bf))
    outs = [rows(512, F32), rows(512, F32), rows(512, bf), rows(512, bf), rows(256, bf), rows(128, F32), trs, trs,
            rows(128, F32), rows(128, F32)]
    return pl.pallas_call(
        _inproj_kernel,
        grid=(n // tm,),
        in_specs=[row(d), _const_spec((d, IN_COLS)), _const_spec(w_vt.shape), tab, tab, tab, tab],
        out_specs=[s for s, _ in outs],
        out_shape=[s for _, s in outs],
        compiler_params=_params("parallel"),
        name="in_projection",
    )(xn, w_cat, w_vt, *tabs)


def _rope_tables(L):
    pos = jnp.arange(L, dtype=F32)
    inv_freq = ROPE_THETA ** (-jnp.arange(0, ROPE_DIMS, 2, dtype=F32) / ROPE_DIMS)
    ang = pos[:, None] * inv_freq[None, :]
    cos, sin = jnp.cos(ang), jnp.sin(ang)
    ones = jnp.ones((L, HEAD_DIM - ROPE_DIMS), F32)
    zeros = jnp.zeros((L, HEAD_DIM - ROPE_DIMS), F32)
    z8 = jnp.zeros((L, ROPE_HALF), F32)
    c64 = jnp.concatenate([cos, cos, ones], axis=1)
    s1_64 = jnp.concatenate([-sin, z8, zeros], axis=1)
    s2_64 = jnp.concatenate([z8, sin, zeros], axis=1)
    dup = lambda a: jnp.concatenate([a, a], axis=1)
    onehot = (jnp.arange(L)[:, None] // SEL_BLOCK == jnp.arange(MAX_SEL_BLOCKS)[None, :]).astype(F32)
    oh = jnp.concatenate([onehot, jnp.zeros((L, LANES - MAX_SEL_BLOCKS), F32)], axis=1)
    return dup(c64), dup(s1_64), dup(s2_64), oh


def _expand_in_weight(w_in):
    o = np.cumsum((0, POOL_WIDTH, SSM_WIDTH, NSA_WIDTH))
    w_pool, w_ssm = w_in[:, o[0]:o[1]], w_in[:, o[1]:o[2]]
    w_q = w_in[:, o[2]:o[3]] * (LOG2E / math.sqrt(HEAD_DIM))
    kv0 = int(o[3])
    kvs = [w_in[:, kv0 + i * KV_WIDTH: kv0 + (i + 1) * KV_WIDTH] for i in range(6)]
    kc, vc, ks, vs, kw, vw = kvs
    head = lambda a, k: a[:, k * HEAD_DIM:(k + 1) * HEAD_DIM]
    dupl = lambda a: jnp.concatenate([head(a, 0), head(a, 0), head(a, 1), head(a, 1)], axis=1)
    g0 = kv0 + 6 * KV_WIDTH
    gn = w_in[:, g0:g0 + 3 * NSA_HEADS]
    gn = jnp.pad(gn, ((0, 0), (0, LANES - 3 * NSA_HEADS)))
    w_cat = jnp.concatenate([w_pool, w_ssm, w_q, dupl(ks), dupl(kw), kc, vc, gn], axis=1)
    w_vt = jnp.concatenate([vs, vw], axis=1).T
    w_gate = w_in[:, g0 + 3 * NSA_HEADS:]
    return w_cat.astype(MXU_DTYPE), w_vt.astype(MXU_DTYPE), w_gate.astype(MXU_DTYPE)


SSM_LANE_GROUPS = LANES // SSM_GROUP_CH
SSM_TILES = SSM_WIDTH // LANES
SSM_TILE_STATE = SSM_LANE_GROUPS * SSM_STATE
SSM_SLAB = 1024


def _s5_matrices(lam_re, lam_im, log_dt, b_re, b_im, c_re, c_im, d_skip):
    T, Q, GL, C, P = SSM_CHUNK, SSM_TILES, SSM_LANE_GROUPS, SSM_GROUP_CH, SSM_STATE
    hp = lax.Precision.HIGHEST
    step = jnp.exp(log_dt)[None, :, None]
    n = jnp.arange(T + 1, dtype=F32)[:, None, None]
    mag = jnp.exp(n * lam_re[None] * step)
    ang = n * lam_im[None] * step
    pr, pi = mag * jnp.cos(ang), mag * jnp.sin(ang)
    den = lam_re * lam_re + lam_im * lam_im
    n_re, n_im = pr[1] - 1.0, pi[1]
    k_re = (n_re * lam_re + n_im * lam_im) / den
    k_im = (n_im * lam_re - n_re * lam_im) / den
    bb_re = k_re[..., None] * b_re - k_im[..., None] * b_im
    bb_im = k_re[..., None] * b_im + k_im[..., None] * b_re
    ca_re = c_re[None] * pr[:, :, None, :] - c_im[None] * pi[:, :, None, :]
    ca_im = c_re[None] * pi[:, :, None, :] + c_im[None] * pr[:, :, None, :]
    taps = (jnp.einsum('tgcp,gpd->tgcd', ca_re[:T], bb_re, precision=hp)
            - jnp.einsum('tgcp,gpd->tgcd', ca_im[:T], bb_im, precision=hp))
    taps = taps.at[0].add(d_skip[:, :, None] * jnp.eye(C, dtype=F32)[None])
    lag = jnp.arange(T)[None, :] - jnp.arange(T)[:, None]
    toe = taps[jnp.clip(lag, 0, T - 1)] * (lag >= 0)[:, :, None, None, None].astype(F32)
    eye = jnp.eye(GL, dtype=F32)
    tile = lambda a, ax: a.reshape(a.shape[:ax] + (Q, GL) + a.shape[ax + 1:])
    toe6 = tile(toe, 2).transpose(2, 0, 3, 5, 1, 4)
    w_toe = (toe6[:, :, :, :, :, None, :] * eye[None, None, :, None, None, :, None]).reshape(
        Q, T * LANES, T * LANES)
    n_rev = (T - 1) - jnp.arange(T, dtype=F32)[:, None, None]
    mag_rev = jnp.exp(n_rev * lam_re[None] * step)
    rev_r = mag_rev * jnp.cos(n_rev * lam_im[None] * step)
    rev_i = mag_rev * jnp.sin(n_rev * lam_im[None] * step)
    bc_re = rev_r[..., None] * bb_re[None] - rev_i[..., None] * bb_im[None]
    bc_im = rev_r[..., None] * bb_im[None] + rev_i[..., None] * bb_re[None]
    to_in = lambda a: (tile(a, 1).transpose(1, 0, 2, 4, 3)[:, :, :, :, None, :]
                       * eye[None, None, :, None, :, None]).reshape(Q, T * LANES, SSM_TILE_STATE)
    w_in = jnp.concatenate([to_in(bc_re), to_in(bc_im)], axis=2)
    to_out = lambda a: (tile(a, 1).transpose(1, 2, 4, 0, 3)[:, :, :, :, None, :]
                        * eye[None, :, None, None, :, None]).reshape(Q, SSM_TILE_STATE, T * LANES)
    w_out = jnp.concatenate([to_out(ca_re[1:]), to_out(-ca_im[1:])], axis=1)
    a_r = pr[T].reshape(Q, 1, SSM_TILE_STATE)
    a_i = pi[T].reshape(Q, 1, SSM_TILE_STATE)
    cast = lambda a: a.astype(MXU_DTYPE)
    return cast(w_toe), cast(w_in), cast(w_out), a_r, a_i


def _s5_kernel(u_ref, wt_ref, win_ref, wout_ref, ar_ref, ai_ref, z_ref, lhs_ref, s_ref, h_ref, carry_ref,
               *, nb, ncs):
    T, NT = SSM_CHUNK, SSM_TILE_STATE // LANES

    @pl.when(pl.program_id(1) == 0)
    def _():
        carry_ref[...] = jnp.zeros_like(carry_ref)

    for b in range(nb):
        for t in range(T):
            lhs_ref[b * ncs:(b + 1) * ncs, t * LANES:(t + 1) * LANES] = (
                u_ref[b, pl.ds(t, ncs, stride=T), :].astype(MXU_DTYPE))
    lhs = lhs_ref[...]
    s = _dot(lhs, win_ref[...])
    for j in range(2 * NT):
        s_ref[j] = s[:, j * LANES:(j + 1) * LANES]
    ar = [jnp.broadcast_to(ar_ref[:, j * LANES:(j + 1) * LANES], (nb, LANES)) for j in range(NT)]
    ai = [jnp.broadcast_to(ai_ref[:, j * LANES:(j + 1) * LANES], (nb, LANES)) for j in range(NT)]

    def body(c, carry):
        rows = pl.ds(c, nb, stride=ncs)
        out = []
        for j in range(NT):
            hr, hi = carry[j]
            h_ref[j, rows, :] = hr
            h_ref[NT + j, rows, :] = hi
            out.append((ar[j] * hr - ai[j] * hi + s_ref[j, rows, :],
                        ar[j] * hi + ai[j] * hr + s_ref[NT + j, rows, :]))
        return tuple(out)

    init = tuple((carry_ref[j], carry_ref[NT + j]) for j in range(NT))
    last = lax.fori_loop(0, ncs, body, init)
    for j in range(NT):
        carry_ref[j] = last[j][0]
        carry_ref[NT + j] = last[j][1]
    h = jnp.concatenate([h_ref[j] for j in range(2 * NT)], axis=1).astype(MXU_DTYPE)
    y = _dot(lhs, wt_ref[...]) + _dot(h, wout_ref[...])
    z = _gelu_tanh(y)
    for b in range(nb):
        for t in range(T):
            z_ref[b, pl.ds(t, ncs, stride=T), :] = z[b * ncs:(b + 1) * ncs, t * LANES:(t + 1) * LANES]


def _s5_mixer(u_ssm, mats, B, L):
    T, Q, NS = SSM_CHUNK, SSM_TILES, SSM_TILE_STATE
    slab = min(SSM_SLAB, L)
    ncs = slab // T
    rows = B * ncs
    per_tile = lambda *s: pl.BlockSpec((None,) + s, lambda q, i: (q,) + (0,) * len(s), pipeline_mode=pl.Buffered(1))
    seq = pl.BlockSpec((B, slab, LANES), lambda q, i: (0, i, q))
    z = pl.pallas_call(
        functools.partial(_s5_kernel, nb=B, ncs=ncs),
        grid=(Q, L // slab),
        in_specs=[seq, per_tile(T * LANES, T * LANES), per_tile(T * LANES, 2 * NS), per_tile(2 * NS, T * LANES),
                  per_tile(1, NS), per_tile(1, NS)],
        out_specs=seq,
        out_shape=jax.ShapeDtypeStruct((B, L, SSM_WIDTH), F32),
        scratch_shapes=[pltpu.VMEM((rows, T * LANES), MXU_DTYPE), pltpu.VMEM((2 * NS // LANES, rows, LANES), F32),
                        pltpu.VMEM((2 * NS // LANES, rows, LANES), F32), pltpu.VMEM((2 * NS // LANES, B, LANES), F32)],
        compiler_params=_params("parallel", "arbitrary"),
        name="s5_mixer",
    )(u_ssm.reshape(B, L, SSM_WIDTH), *mats)
    return z.reshape(B * L, SSM_WIDTH)


def _compress_kernel(kc_ref, vc_ref, pek_ref, pev_ref, wk1_ref, wk2_ref, wv1_ref, wv2_ref, ko_ref, vo_ref, *, nrow):
    def one(src_ref, pe_ref, w1_ref, w2_ref, out_ref, transposed):
        first = jnp.zeros((nrow, NSA_KV_HEADS * CMP_HIDDEN), F32)
        second = jnp.zeros((nrow, NSA_KV_HEADS * CMP_HIDDEN), F32)
        for t in range(CMP_STRIDE):
            x = src_ref[pl.ds(t, nrow, stride=CMP_STRIDE), :]
            first = first + _dot((x + pe_ref[t:t + 1, :]).astype(MXU_DTYPE), w1_ref[t])
            u = CMP_STRIDE + t
            second = second + _dot((x + pe_ref[u:u + 1, :]).astype(MXU_DTYPE), w1_ref[u])
        hid = first + pltpu.roll(second, nrow - 1, 0)
        act = _gelu_tanh(hid).astype(MXU_DTYPE)
        for k in range(NSA_KV_HEADS):
            res = _nt_dot(w2_ref[k], act) if transposed else _dot(act, w2_ref[k])
            out_ref[k] = res.astype(out_ref.dtype)

    one(kc_ref, pek_ref, wk1_ref, wk2_ref, ko_ref, False)
    one(vc_ref, pev_ref, wv1_ref, wv2_ref, vo_ref, True)


def _compress_weights(pe, w1, w2, transposed):
    w1r = w1.reshape(CMP_BLOCK, HEAD_DIM, CMP_HIDDEN)
    eye = jnp.eye(NSA_KV_HEADS, dtype=w1.dtype)
    w1x = (w1r[:, None, :, None, :] * eye[None, :, None, :, None]).reshape(
        CMP_BLOCK, KV_WIDTH, NSA_KV_HEADS * CMP_HIDDEN)
    pe2 = jnp.concatenate([pe] * NSA_KV_HEADS, axis=1)
    w2k = []
    for k in range(NSA_KV_HEADS):
        cols = w2.T if transposed else jnp.concatenate([w2, w2], axis=1)
        z = jnp.zeros_like(cols)
        parts = [cols if h == k else z for h in range(NSA_KV_HEADS)]
        w2k.append(jnp.concatenate(parts, axis=1 if transposed else 0))
    return pe2.astype(F32), w1x.astype(MXU_DTYPE), jnp.stack(w2k).astype(MXU_DTYPE)


def _compress(kc, vc, wk, wv, B, L):
    nrow = L // CMP_STRIDE
    src = pl.BlockSpec((L, KV_WIDTH), lambda b: (b, 0))
    out = pl.BlockSpec((None, NSA_KV_HEADS, nrow, LANES), lambda b: (b, 0, 0, 0))
    pek, wk1, wk2 = wk
    pev, wv1, wv2 = wv
    consts = [pek, pev, wk1, wk2, wv1, wv2]
    out_t = pl.BlockSpec((None, NSA_KV_HEADS, HEAD_DIM, nrow), lambda b: (b, 0, 0, 0))
    shape = jax.ShapeDtypeStruct((B, NSA_KV_HEADS, nrow, LANES), MXU_DTYPE)
    shape_t = jax.ShapeDtypeStruct((B, NSA_KV_HEADS, HEAD_DIM, nrow), MXU_DTYPE)
    return pl.pallas_call(
        functools.partial(_compress_kernel, nrow=nrow),
        grid=(B,),
        in_specs=[src, src] + [_const_spec(c.shape) for c in consts],
        out_specs=[out, out_t],
        out_shape=[shape, shape_t],
        compiler_params=_params("parallel"),
        name="kv_compress",
    )(kc, vc, *consts)


def _nsa_kernel(q_ref, gn_ref, kcmp_ref, vcmp_ref, ksx_ref, vst_ref, kwd_ref, vwt_ref, ov_ref,
                o_ref, *, tq, tkb, ncmp, topk):
    q0 = pl.multiple_of(pl.program_id(1) * tq, tq)
    rows4 = NSA_GROUP * tq
    lo = lax.broadcasted_iota(jnp.int32, (tq, LANES), 1) < HEAD_DIM
    qt = [q_ref[:, j * LANES:(j + 1) * LANES] for j in range(NSA_HEADS // 2)]
    qzero = jnp.zeros_like(qt[0])
    qm = [jnp.where(lo, qt[h // 2], qzero) if h % 2 == 0 else jnp.where(lo, qzero, qt[h // 2])
          for h in range(NSA_HEADS)]
    q_grp = [jnp.concatenate(qm[NSA_GROUP * k: NSA_GROUP * (k + 1)], axis=0) for k in range(NSA_KV_HEADS)]

    def query_pos(nkeys):
        t = q0 + lax.broadcasted_iota(jnp.int32, (nkeys, tq), 1)
        return jnp.concatenate([t] * NSA_GROUP, axis=1)

    def key_idx(nkeys):
        return lax.broadcasted_iota(jnp.int32, (nkeys, rows4), 0)

    cvalid = key_idx(ncmp) * CMP_STRIDE + (CMP_BLOCK - 1) <= query_pos(ncmp)
    o_cmp = []
    bias = []
    j_s = lax.broadcasted_iota(jnp.int32, (MAX_SEL_BLOCKS, tq), 0)
    cur = (q0 + lax.broadcasted_iota(jnp.int32, (MAX_SEL_BLOCKS, tq), 1)) // SEL_BLOCK
    forced = (j_s == 0) | (j_s == cur) | (j_s == cur - 1)
    for k in range(NSA_KV_HEADS):
        sm = jnp.where(cvalid, _nt_dot(kcmp_ref[k], q_grp[k]), NEG)
        m = jnp.max(sm, axis=0, keepdims=True)
        e = jnp.where(cvalid, jnp.exp2(sm - m), 0.0)
        den = jnp.sum(e, axis=0, keepdims=True)
        p = e / jnp.where(den > 0.0, den, 1.0)
        o_cmp.append(_dot(vcmp_ref[k], p.astype(MXU_DTYPE)))
        psum = p[:, 0:tq]
        for g in range(1, NSA_GROUP):
            psum = psum + p[:, g * tq:(g + 1) * tq]
        p_hi = psum.astype(MXU_DTYPE)
        p_lo = (psum - p_hi.astype(F32)).astype(MXU_DTYPE)
        imp_t = _dot(ov_ref[...], p_hi) + _dot(ov_ref[...], p_lo)
        score = jnp.where(forced, FORCE_SCORE, jnp.where(j_s <= cur, imp_t, -FORCE_SCORE))
        nblk = MAX_SEL_BLOCKS // SUBLANES
        sblk = [score[r * SUBLANES:(r + 1) * SUBLANES] for r in range(nblk)]
        jblk = j_s[0:SUBLANES]
        cnt = [jnp.zeros((SUBLANES, tq), F32) for _ in range(nblk)]
        for i in range(MAX_SEL_BLOCKS):
            ri = jnp.broadcast_to(score[i:i + 1, :], (SUBLANES, tq))
            for r in range(nblk):
                if r * SUBLANES > i:
                    beats = ri >= sblk[r]
                elif r * SUBLANES + SUBLANES - 1 < i:
                    beats = ri > sblk[r]
                else:
                    tie = jnp.where(jblk + r * SUBLANES > i, 1.0, 0.0)
                    beats = jnp.where(ri > sblk[r], 1.0, jnp.where(ri == sblk[r], tie, 0.0)) > 0.5
                cnt[r] = cnt[r] + jnp.where(beats, 1.0, 0.0)
        drop = jnp.concatenate([jnp.where(c < float(topk), 0.0, 1.0) for c in cnt] * 2, axis=0)
        bias.append((jnp.transpose(drop) * SEL_BIAS).astype(MXU_DTYPE))

    nfull = q0 // tkb
    q_sel = [jnp.concatenate([q_grp[k], jnp.concatenate([bias[k]] * NSA_GROUP, axis=0)], axis=1)
             for k in range(NSA_KV_HEADS)]
    t_d = query_pos(tkb)
    k_d = key_idx(tkb)

    def sel_step(kb, carry, diagonal):
        off = pl.multiple_of(kb * tkb, tkb)
        out = []
        for k in range(NSA_KV_HEADS):
            m, l, acc = carry[k]
            s = _nt_dot(ksx_ref[pl.ds(off, tkb), 2 * k * LANES:(2 * k + 2) * LANES], q_sel[k])
            if diagonal:
                s = jnp.where(k_d + off <= t_d, s, NEG)
            m_new = jnp.maximum(m, jnp.max(s, axis=0, keepdims=True))
            alpha = jnp.exp2(m - m_new)
            p = jnp.exp2(s - m_new)
            l = alpha * l + jnp.sum(p, axis=0, keepdims=True)
            v_t = vst_ref[k * HEAD_DIM:(k + 1) * HEAD_DIM, pl.ds(off, tkb)]
            acc = alpha * acc + _dot(v_t, p.astype(MXU_DTYPE))
            out.append((m_new, l, acc))
        return tuple(out)

    init = tuple((jnp.full((1, rows4), NEG, F32), jnp.zeros((1, rows4), F32), jnp.zeros((HEAD_DIM, rows4), F32))
                 for _ in range(NSA_KV_HEADS))
    carry = lax.fori_loop(0, nfull, lambda kb, c: sel_step(kb, c, False), init)
    o_sel = [acc / l for (_, l, acc) in sel_step(nfull, carry, True)]

    span = WINDOW + tq
    start = pl.multiple_of(jnp.maximum(q0 - WINDOW, 0), tq)
    k_w = start + key_idx(span)
    t_w = query_pos(span)
    wvalid = (k_w <= t_w) & (t_w - k_w < WINDOW)
    o_win = []
    for k in range(NSA_KV_HEADS):
        s = _nt_dot(kwd_ref[pl.ds(start, span), k * LANES:(k + 1) * LANES], q_grp[k])
        sm = jnp.where(wvalid, s, NEG)
        e = jnp.exp2(sm - jnp.max(sm, axis=0, keepdims=True))
        den = jnp.sum(e, axis=0, keepdims=True)
        v_t = vwt_ref[k * HEAD_DIM:(k + 1) * HEAD_DIM, pl.ds(start, span)]
        o_win.append(_dot(v_t, e.astype(MXU_DTYPE)) / den)

    g_t = jnp.transpose(gn_ref[...])
    for j in range(NSA_HEADS // 2):
        halves = []
        for h in (2 * j, 2 * j + 1):
            k, cols = h // NSA_GROUP, slice((h % NSA_GROUP) * tq, (h % NSA_GROUP + 1) * tq)
            gate = lambda br: g_t[3 * h + br: 3 * h + br + 1, :]
            halves.append(gate(0) * o_cmp[k][:, cols] + gate(1) * o_sel[k][:, cols] + gate(2) * o_win[k][:, cols])
        o_ref[:, j * LANES:(j + 1) * LANES] = jnp.transpose(jnp.concatenate(halves, axis=0)).astype(o_ref.dtype)


def _overlap_t(ncmp, n_sel):
    cs = np.arange(ncmp)[None, :] * CMP_STRIDE
    ss = np.arange(MAX_SEL_BLOCKS)[:, None] * SEL_BLOCK
    ov = np.minimum(cs + CMP_BLOCK, ss + SEL_BLOCK) - np.maximum(cs, ss)
    ov = np.maximum(ov, 0) / CMP_STRIDE
    ov[n_sel:] = 0.0
    ov[:, (ncmp - 1):] = 0.0
    return jnp.asarray(ov, dtype=MXU_DTYPE)


def _nsa_attention(q2, gn, kcmp, vcmp_t, ksx, vs_t, kwd, vw_t, B, L):
    tq = Q_TILE
    tkb = min(SEL_KEY_TILE, L)
    ncmp = L // CMP_STRIDE
    n_sel = L // SEL_BLOCK
    assert n_sel <= MAX_SEL_BLOCKS and L >= WINDOW + tq and L % tkb == 0 and tkb % tq == 0
    nq = L // tq
    seq = lambda a: a.reshape(B, L, a.shape[-1])
    kv_spec = lambda w: pl.BlockSpec((None, L, w), lambda b, i: (b, 0, 0))
    vt_spec = pl.BlockSpec((None, KV_WIDTH, L), lambda b, i: (b, 0, 0))
    kc_spec = pl.BlockSpec((None, NSA_KV_HEADS, ncmp, LANES), lambda b, i: (b, 0, 0, 0))
    vc_spec = pl.BlockSpec((None, NSA_KV_HEADS, HEAD_DIM, ncmp), lambda b, i: (b, 0, 0, 0))
    row = lambda w: pl.BlockSpec((tq, w), lambda b, i: (b * nq + i, 0))
    ovt = _overlap_t(ncmp, n_sel)
    return pl.pallas_call(
        functools.partial(_nsa_kernel, tq=tq, tkb=tkb, ncmp=ncmp, topk=min(SEL_TOPK, n_sel)),
        grid=(B, nq),
        in_specs=[row(NSA_WIDTH), row(LANES), kc_spec, vc_spec, kv_spec(4 * LANES), vt_spec,
                  kv_spec(2 * LANES), vt_spec, _const_spec(ovt.shape)],
        out_specs=row(NSA_WIDTH),
        out_shape=jax.ShapeDtypeStruct((B * L, NSA_WIDTH), MXU_DTYPE),
        compiler_params=_params("parallel", "arbitrary"),
        name="nsa_attention",
    )(q2, gn, kcmp, vcmp_t, seq(ksx), vs_t, seq(kwd), vw_t, ovt)


POOL_HALO = 16


def _merge_kernel(x_ref, up_ref, halo_ref, zs_ref, yn_ref, wg_ref, wpool_ref, pscale_ref, wglu_ref, bglu_ref,
                  wup_p_ref, wup_s_ref, wup_n_ref, wout_ref, g_ref, b_ref, o_ref, *, tm, nl):
    x = x_ref[...]
    xb = x.astype(MXU_DTYPE)
    first_tile = (pl.program_id(0) % nl) == 0
    halo = jnp.where(first_tile, 0.0, halo_ref[...])
    ext = jnp.concatenate([halo, up_ref[...]], axis=0)
    t_in_seq = (pl.program_id(0) % nl) * tm + lax.broadcasted_iota(jnp.int32, (tm, 1), 0)
    y_pool = []
    for gi, w in enumerate(POOL_WINDOWS):
        lanes = slice(gi * POOL_GROUP_CH, (gi + 1) * POOL_GROUP_CH)
        e = ext[:, lanes]
        span = 1
        while span < w:
            e = e + pltpu.roll(e, span, 0)
            span *= 2
        cnt = jnp.minimum(t_in_seq + 1, w).astype(F32)
        z = e[POOL_HALO:] / cnt - ext[POOL_HALO:, lanes]
        y_pool.append(_dot(z.astype(MXU_DTYPE), wpool_ref[gi]))
    y_pool = jnp.concatenate(y_pool, axis=1) * pscale_ref[...]
    zs = zs_ref[...]
    y_ssm = zs * jax.nn.sigmoid(_dot(zs.astype(MXU_DTYPE), wglu_ref[...]) + bglu_ref[...])
    d = x.shape[1]
    merged = jnp.zeros_like(x)
    for br, (y, w_ref) in enumerate(((y_pool.astype(MXU_DTYPE), wup_p_ref), (y_ssm.astype(MXU_DTYPE), wup_s_ref),
                                     (yn_ref[...], wup_n_ref))):
        gate = jax.nn.sigmoid(_dot(xb, wg_ref[:, br * d:(br + 1) * d]))
        merged = merged + gate * _dot(y, w_ref[...])
    o_ref[...] = _layer_norm(ALPHA * x + _dot(merged.astype(MXU_DTYPE), wout_ref[...]), g_ref[...], b_ref[...])


def _merge(xn, u_pool, z_ssm, y_nsa, w, L):
    n, d = xn.shape
    tm = ROW_TILE // 2
    nl = L // tm
    row = lambda wd: pl.BlockSpec((tm, wd), lambda i: (i, 0))
    halo = pl.BlockSpec((POOL_HALO, POOL_WIDTH), lambda i: (jnp.maximum(i * (tm // POOL_HALO) - 1, 0), 0))
    consts = [w["w_gate"], w["w_pool"], w["pool_scale"], w["w_glu"], w["b_glu"], w["w_up_pool"], w["w_up_ssm"],
              w["w_up_nsa"], w["w_out"], w["ln1_g"], w["ln1_b"]]
    return pl.pallas_call(
        functools.partial(_merge_kernel, tm=tm, nl=nl),
        grid=(n // tm,),
        in_specs=[row(d), row(POOL_WIDTH), halo, row(SSM_WIDTH), row(NSA_WIDTH)] + [_const_spec(c.shape) for c in consts],
        out_specs=row(d),
        out_shape=jax.ShapeDtypeStruct((n, d), F32),
        compiler_params=_params("parallel"),
        name="gated_merge",
    )(xn, u_pool, u_pool, z_ssm, y_nsa, *consts)


FF_CHUNK = 1024


def _ffn_kernel(x_ref, w1_ref, w2_ref, g_ref, b_ref, o_ref):
    x = x_ref[...]
    xb = x.astype(MXU_DTYPE)
    acc = jnp.zeros_like(x)
    for c in range(D_FF // FF_CHUNK):
        cols = slice(c * FF_CHUNK, (c + 1) * FF_CHUNK)
        h = jnp.maximum(_dot(xb, w1_ref[:, cols]), 0.0)
        acc = acc + _dot((h * h).astype(MXU_DTYPE), w2_ref[cols, :])
    o_ref[...] = _layer_norm(ALPHA * x + acc, g_ref[...], b_ref[...])


def _ffn(x1, w1, w2, g, b):
    n, d = x1.shape
    tm = ROW_TILE
    row = pl.BlockSpec((tm, d), lambda i: (i, 0))
    return pl.pallas_call(
        _ffn_kernel,
        grid=(n // tm,),
        in_specs=[row, _const_spec(w1.shape), _const_spec(w2.shape), _const_spec((1, d)), _const_spec((1, d))],
        out_specs=row,
        out_shape=jax.ShapeDtypeStruct((n, d), F32),
        compiler_params=_params("parallel"),
        name="relu2_mlp",
    )(x1, w1, w2, g.reshape(1, d), b.reshape(1, d))


def _hybrid_layer(xn, tabs, B, L, p):
    w_cat, w_vt, w_gate = _expand_in_weight(p["w_in"])
    (u_pool, u_ssm, q2, ksx, kwd, kc, vs_t, vw_t, vc, gn) = _in_projection(xn, w_cat, w_vt, tabs, B, L)
    mats = _s5_matrices(p["ssm_lam_re"], p["ssm_lam_im"], p["ssm_log_dt"], p["ssm_b_re"], p["ssm_b_im"],
                        p["ssm_c_re"], p["ssm_c_im"], p["ssm_d"])
    z_ssm = _s5_mixer(u_ssm, mats, B, L)
    kcmp, vcmp_t = _compress(kc, vc, _compress_weights(p["cmp_pe_k"], p["cmp_wk1"], p["cmp_wk2"], False),
                             _compress_weights(p["cmp_pe_v"], p["cmp_wv1"], p["cmp_wv2"], True), B, L)
    y_nsa = _nsa_attention(q2, gn, kcmp, vcmp_t, ksx, vs_t, kwd, vw_t, B, L)
    cast = lambda a: a.astype(MXU_DTYPE)
    row = lambda a: a.reshape(1, -1).astype(F32)
    w = dict(w_gate=w_gate, w_pool=cast(p["w_pool"]), pool_scale=row(p["pool_scale"]), w_glu=cast(p["w_glu"]),
             b_glu=row(p["b_glu"]), w_up_pool=cast(p["w_up_pool"]), w_up_ssm=cast(p["w_up_ssm"]),
             w_up_nsa=cast(p["w_up_nsa"]), w_out=cast(p["w_out"]), ln1_g=row(p["ln1_g"]), ln1_b=row(p["ln1_b"]))
    x1 = _merge(xn, u_pool, z_ssm, y_nsa, w, L)
    return _ffn(x1, cast(p["w_ff1"]), cast(p["w_ff2"]), p["ln2_g"], p["ln2_b"])


_LAYER_PARAMS = ("w_in", "w_pool", "pool_scale", "ssm_lam_re", "ssm_lam_im", "ssm_log_dt", "ssm_b_re", "ssm_b_im",
                 "ssm_c_re", "ssm_c_im", "ssm_d", "w_glu", "b_glu", "cmp_pe_k", "cmp_pe_v", "cmp_wk1", "cmp_wk2",
                 "cmp_wv1", "cmp_wv2", "w_up_pool", "w_up_ssm", "w_up_nsa", "w_out", "ln1_g", "ln1_b", "w_ff1",
                 "w_ff2", "ln2_g", "ln2_b")


def kernel(x, ln_in_g, ln_in_b, w_in, w_pool, pool_scale, ssm_lam_re, ssm_lam_im, ssm_log_dt, ssm_b_re, ssm_b_im, ssm_c_re, ssm_c_im, ssm_d, w_glu, b_glu, cmp_pe_k, cmp_pe_v, cmp_wk1, cmp_wk2, cmp_wv1, cmp_wv2, w_up_pool, w_up_ssm, w_up_nsa, w_out, ln1_g, ln1_b, w_ff1, w_ff2, ln2_g, ln2_b):
    B, L, D = x.shape
    assert D == D_MODEL and L % ROW_TILE == 0
    stacked = dict(zip(_LAYER_PARAMS, (w_in, w_pool, pool_scale, ssm_lam_re, ssm_lam_im, ssm_log_dt, ssm_b_re,
                                       ssm_b_im, ssm_c_re, ssm_c_im, ssm_d, w_glu, b_glu, cmp_pe_k, cmp_pe_v,
                                       cmp_wk1, cmp_wk2, cmp_wv1, cmp_wv2, w_up_pool, w_up_ssm, w_up_nsa, w_out,
                                       ln1_g, ln1_b, w_ff1, w_ff2, ln2_g, ln2_b)))
    tabs = _rope_tables(L)
    xn = _entry_norm(x.reshape(B * L, D), ln_in_g, ln_in_b)
    for i in range(w_in.shape[0]):
        xn = _hybrid_layer(xn, tabs, B, L, {k: v[i] for k, v in stacked.items()})
    return xn.reshape(B, L, D)
```

```python
import functools
import math

import numpy as np
import jax
import jax.numpy as jnp
from jax import lax
from jax.experimental import pallas as pl
from jax.experimental.pallas import tpu as pltpu

D_MODEL = 1024
DEPTH = 2
POOL_WIDTH = D_MODEL // 2
POOL_GROUPS = 4
POOL_WINDOWS = (2, 4, 8, 16)
POOL_GROUP_CH = POOL_WIDTH // POOL_GROUPS
SSM_WIDTH = D_MODEL // 2
SSM_GROUP_CH = 16
SSM_GROUPS = SSM_WIDTH // SSM_GROUP_CH
SSM_STATE = 64
NSA_HEADS = 8
NSA_KV_HEADS = 2
NSA_GROUP = NSA_HEADS // NSA_KV_HEADS
HEAD_DIM = 64
NSA_WIDTH = NSA_HEADS * HEAD_DIM
KV_WIDTH = NSA_KV_HEADS * HEAD_DIM
CMP_BLOCK = 32
CMP_STRIDE = 16
CMP_HIDDEN = 2 * HEAD_DIM
SEL_BLOCK = 64
SEL_TOPK = 16
WINDOW = 512
ROPE_THETA = 500000.0
ROPE_DIMS = HEAD_DIM // 4
ROPE_HALF = ROPE_DIMS // 2
NEG = -1e30
FORCE_SCORE = 1e6
D_FF = 4 * D_MODEL
ALPHA = (2 * DEPTH) ** 0.25
LN_EPS = 1e-5

LANES = 128
SUBLANES = 8
VMEM_LIMIT = 56 * 1024 * 1024

MXU_DTYPE = jnp.bfloat16
F32 = jnp.float32

LOG2E = math.log2(math.e)
SEL_BIAS = -(2.0 ** 100)
SSM_CHUNK = 8
MAX_SEL_BLOCKS = 64

ROW_TILE = 512
Q_TILE = 256
SEL_KEY_TILE = 512


def _nt_dot(a, b):
    return lax.dot_general(a, b, (((1,), (1,)), ((), ())), preferred_element_type=F32)


def _dot(a, b):
    return jnp.dot(a, b, preferred_element_type=F32)


def _gelu_tanh(x):
    return x * (0.5 * (1.0 + jnp.tanh(math.sqrt(2.0 / math.pi) * (x + 0.044715 * (x * x * x)))))


def _layer_norm(xf, g, b):
    mu = jnp.mean(xf, axis=-1, keepdims=True)
    xc = xf - mu
    var = jnp.mean(xc * xc, axis=-1, keepdims=True)
    return xc * lax.rsqrt(var + LN_EPS) * g + b


def _params(*sem):
    return pltpu.CompilerParams(dimension_semantics=sem, vmem_limit_bytes=VMEM_LIMIT)


def _const_spec(shape):
    nd = len(shape)
    return pl.BlockSpec(shape, lambda *_: (0,) * nd, pipeline_mode=pl.Buffered(1))


def _ln_kernel(x_ref, g_ref, b_ref, o_ref):
    o_ref[...] = _layer_norm(x_ref[...], g_ref[...], b_ref[...])


def _entry_norm(x2, g, b):
    n, d = x2.shape
    tm = ROW_TILE
    return pl.pallas_call(
        _ln_kernel,
        grid=(n // tm,),
        in_specs=[pl.BlockSpec((tm, d), lambda i: (i, 0)), _const_spec((1, d)), _const_spec((1, d))],
        out_specs=pl.BlockSpec((tm, d), lambda i: (i, 0)),
        out_shape=jax.ShapeDtypeStruct((n, d), F32),
        compiler_params=_params("parallel"),
        name="entry_norm",
    )(x2, g.reshape(1, d), b.reshape(1, d))


_SEC = {}
_off = 0
for _name, _w in (("pool", 512), ("ssm", 512), ("q", 512), ("ks", 128), ("kw", 128),
                  ("kc", 128), ("vc", 128), ("gn", 128)):
    _SEC[_name] = (_off, _off + _w)
    _off += _w
IN_COLS = _off


def _inproj_kernel(x_ref, w_ref, wvt_ref, cs_ref, s1_ref, s2_ref, oh_ref,
                   up_ref, us_ref, q_ref, ksx_ref, kw_ref, kc_ref, vst_ref, vwt_ref, vc_ref, gn_ref):
    xb = x_ref[...].astype(MXU_DTYPE)

    def proj(name):
        lo, hi = _SEC[name]
        return _dot(xb, w_ref[:, lo:hi])

    cs, s1, s2 = cs_ref[...], s1_ref[...], s2_ref[...]

    def rope(a):
        return a * cs + pltpu.roll(a, LANES - ROPE_HALF, 1) * s1 + pltpu.roll(a, ROPE_HALF, 1) * s2

    def roped_tiles(name):
        acc = proj(name)
        return [rope(acc[:, j * LANES:(j + 1) * LANES]) for j in range(acc.shape[1] // LANES)]

    def store_tiles(ref, tiles):
        for j, t in enumerate(tiles):
            ref[:, j * LANES:(j + 1) * LANES] = t.astype(ref.dtype)

    up_ref[...] = proj("pool")
    us_ref[...] = proj("ssm")
    store_tiles(q_ref, roped_tiles("q"))
    store_tiles(ksx_ref, roped_tiles("ks") + [oh_ref[...]])
    store_tiles(kw_ref, roped_tiles("kw"))
    store_tiles(kc_ref, roped_tiles("kc"))
    vt = _nt_dot(wvt_ref[...], xb)
    vst_ref[...] = vt[0:KV_WIDTH].astype(vst_ref.dtype)
    vwt_ref[...] = vt[KV_WIDTH:2 * KV_WIDTH].astype(vwt_ref.dtype)
    vc_ref[...] = proj("vc")
    gn_ref[...] = jax.nn.sigmoid(proj("gn"))


def _in_projection(xn, w_cat, w_vt, tabs, B, L):
    n, d = xn.shape
    tm = ROW_TILE
    nl = L // tm
    row = lambda w: pl.BlockSpec((tm, w), lambda i: (i, 0))
    tab = pl.BlockSpec((tm, LANES), lambda i: (i % nl, 0))
    tr = pl.BlockSpec((None, KV_WIDTH, tm), lambda i: (i // nl, 0, i % nl))
    bf = MXU_DTYPE
    rows = lambda w, dt: (row(w), jax.ShapeDtypeStruct((n, w), dt))
    trs = (tr, jax.ShapeDtypeStruct((B, KV_WIDTH, L), bf))
    outs = [rows(512, F32), rows(512, F32), rows(512, bf), rows(256, bf), rows(128, bf), rows(128, F32), trs, trs,
            rows(128, F32), rows(128, F32)]
    return pl.pallas_call(
        _inproj_kernel,
        grid=(n // tm,),
        in_specs=[row(d), _const_spec((d, IN_COLS)), _const_spec(w_vt.shape), tab, tab, tab, tab],
        out_specs=[s for s, _ in outs],
        out_shape=[s for _, s in outs],
        compiler_params=_params("parallel"),
        name="in_projection",
    )(xn, w_cat, w_vt, *tabs)


def _rope_tables(L):
    pos = jnp.arange(L, dtype=F32)
    inv_freq = ROPE_THETA ** (-jnp.arange(0, ROPE_DIMS, 2, dtype=F32) / ROPE_DIMS)
    ang = pos[:, None] * inv_freq[None, :]
    cos, sin = jnp.cos(ang), jnp.sin(ang)
    ones = jnp.ones((L, HEAD_DIM - ROPE_DIMS), F32)
    zeros = jnp.zeros((L, HEAD_DIM - ROPE_DIMS), F32)
    z8 = jnp.zeros((L, ROPE_HALF), F32)
    c64 = jnp.concatenate([cos, cos, ones], axis=1)
    s1_64 = jnp.concatenate([-sin, z8, zeros], axis=1)
    s2_64 = jnp.concatenate([z8, sin, zeros], axis=1)
    dup = lambda a: jnp.concatenate([a, a], axis=1)
    onehot = (jnp.arange(L)[:, None] // SEL_BLOCK == jnp.arange(MAX_SEL_BLOCKS)[None, :]).astype(F32)
    oh = jnp.concatenate([onehot, jnp.zeros((L, LANES - MAX_SEL_BLOCKS), F32)], axis=1)
    return dup(c64), dup(s1_64), dup(s2_64), oh


def _expand_in_weight(w_in):
    o = np.cumsum((0, POOL_WIDTH, SSM_WIDTH, NSA_WIDTH))
    w_pool, w_ssm = w_in[:, o[0]:o[1]], w_in[:, o[1]:o[2]]
    w_q = w_in[:, o[2]:o[3]] * (LOG2E / math.sqrt(HEAD_DIM))
    q_heads = [w_q[:, h * HEAD_DIM:(h + 1) * HEAD_DIM] for h in range(NSA_HEADS)]
    w_q = jnp.concatenate([q_heads[h] for j in range(NSA_GROUP) for h in (j, NSA_GROUP + j)], axis=1)
    kv0 = int(o[3])
    kvs = [w_in[:, kv0 + i * KV_WIDTH: kv0 + (i + 1) * KV_WIDTH] for i in range(6)]
    kc, vc, ks, vs, kw, vw = kvs
    g0 = kv0 + 6 * KV_WIDTH
    gn = w_in[:, g0:g0 + 3 * NSA_HEADS]
    gn = jnp.pad(gn, ((0, 0), (0, LANES - 3 * NSA_HEADS)))
    w_cat = jnp.concatenate([w_pool, w_ssm, w_q, ks, kw, kc, vc, gn], axis=1)
    w_vt = jnp.concatenate([vs, vw], axis=1).T
    w_gate = w_in[:, g0 + 3 * NSA_HEADS:]
    return w_cat.astype(MXU_DTYPE), w_vt.astype(MXU_DTYPE), w_gate.astype(MXU_DTYPE)


SSM_LANE_GROUPS = LANES // SSM_GROUP_CH
SSM_TILES = SSM_WIDTH // LANES
SSM_TILE_STATE = SSM_LANE_GROUPS * SSM_STATE
SSM_SLAB = 1024


def _s5_matrices(lam_re, lam_im, log_dt, b_re, b_im, c_re, c_im, d_skip):
    T, Q, GL, C, P = SSM_CHUNK, SSM_TILES, SSM_LANE_GROUPS, SSM_GROUP_CH, SSM_STATE
    hp = lax.Precision.HIGHEST
    step = jnp.exp(log_dt)[None, :, None]
    n = jnp.arange(T + 1, dtype=F32)[:, None, None]
    mag = jnp.exp(n * lam_re[None] * step)
    ang = n * lam_im[None] * step
    pr, pi = mag * jnp.cos(ang), mag * jnp.sin(ang)
    den = lam_re * lam_re + lam_im * lam_im
    n_re, n_im = pr[1] - 1.0, pi[1]
    k_re = (n_re * lam_re + n_im * lam_im) / den
    k_im = (n_im * lam_re - n_re * lam_im) / den
    bb_re = k_re[..., None] * b_re - k_im[..., None] * b_im
    bb_im = k_re[..., None] * b_im + k_im[..., None] * b_re
    ca_re = c_re[None] * pr[:, :, None, :] - c_im[None] * pi[:, :, None, :]
    ca_im = c_re[None] * pi[:, :, None, :] + c_im[None] * pr[:, :, None, :]
    taps = (jnp.einsum('tgcp,gpd->tgcd', ca_re[:T], bb_re, precision=hp)
            - jnp.einsum('tgcp,gpd->tgcd', ca_im[:T], bb_im, precision=hp))
    taps = taps.at[0].add(d_skip[:, :, None] * jnp.eye(C, dtype=F32)[None])
    lag = jnp.arange(T)[None, :] - jnp.arange(T)[:, None]
    toe = taps[jnp.clip(lag, 0, T - 1)] * (lag >= 0)[:, :, None, None, None].astype(F32)
    eye = jnp.eye(GL, dtype=F32)
    tile = lambda a, ax: a.reshape(a.shape[:ax] + (Q, GL) + a.shape[ax + 1:])
    toe6 = tile(toe, 2).transpose(2, 0, 3, 5, 1, 4)
    w_toe = (toe6[:, :, :, :, :, None, :] * eye[None, None, :, None, None, :, None]).reshape(
        Q, T * LANES, T * LANES)
    n_rev = (T - 1) - jnp.arange(T, dtype=F32)[:, None, None]
    mag_rev = jnp.exp(n_rev * lam_re[None] * step)
    rev_r = mag_rev * jnp.cos(n_rev * lam_im[None] * step)
    rev_i = mag_rev * jnp.sin(n_rev * lam_im[None] * step)
    bc_re = rev_r[..., None] * bb_re[None] - rev_i[..., None] * bb_im[None]
    bc_im = rev_r[..., None] * bb_im[None] + rev_i[..., None] * bb_re[None]
    to_in = lambda a: (tile(a, 1).transpose(1, 0, 2, 4, 3)[:, :, :, :, None, :]
                       * eye[None, None, :, None, :, None]).reshape(Q, T * LANES, SSM_TILE_STATE)
    w_in = jnp.concatenate([to_in(bc_re), to_in(bc_im)], axis=2)
    to_out = lambda a: (tile(a, 1).transpose(1, 2, 4, 0, 3)[:, :, :, :, None, :]
                        * eye[None, :, None, None, :, None]).reshape(Q, SSM_TILE_STATE, T * LANES)
    w_out = jnp.concatenate([to_out(ca_re[1:]), to_out(-ca_im[1:])], axis=1)
    a_r = pr[T].reshape(Q, 1, SSM_TILE_STATE)
    a_i = pi[T].reshape(Q, 1, SSM_TILE_STATE)
    cast = lambda a: a.astype(MXU_DTYPE)
    return cast(w_toe), cast(w_in), cast(w_out), a_r, a_i


def _s5_kernel(u_ref, wt_ref, win_ref, wout_ref, ar_ref, ai_ref, z_ref, lhs_ref, s_ref, h_ref, carry_ref,
               *, nb, ncs):
    T, NT = SSM_CHUNK, SSM_TILE_STATE // LANES

    @pl.when(pl.program_id(1) == 0)
    def _():
        carry_ref[...] = jnp.zeros_like(carry_ref)

    for b in range(nb):
        for t in range(T):
            lhs_ref[b * ncs:(b + 1) * ncs, t * LANES:(t + 1) * LANES] = (
                u_ref[b, pl.ds(t, ncs, stride=T), :].astype(MXU_DTYPE))
    lhs = lhs_ref[...]
    s = _dot(lhs, win_ref[...])
    for j in range(2 * NT):
        s_ref[j] = s[:, j * LANES:(j + 1) * LANES]
    ar = [jnp.broadcast_to(ar_ref[:, j * LANES:(j + 1) * LANES], (nb, LANES)) for j in range(NT)]
    ai = [jnp.broadcast_to(ai_ref[:, j * LANES:(j + 1) * LANES], (nb, LANES)) for j in range(NT)]

    def body(c, carry):
        rows = pl.ds(c, nb, stride=ncs)
        out = []
        for j in range(NT):
            hr, hi = carry[j]
            h_ref[j, rows, :] = hr
            h_ref[NT + j, rows, :] = hi
            out.append((ar[j] * hr - ai[j] * hi + s_ref[j, rows, :],
                        ar[j] * hi + ai[j] * hr + s_ref[NT + j, rows, :]))
        return tuple(out)

    init = tuple((carry_ref[j], carry_ref[NT + j]) for j in range(NT))
    last = lax.fori_loop(0, ncs, body, init)
    for j in range(NT):
        carry_ref[j] = last[j][0]
        carry_ref[NT + j] = last[j][1]
    h = jnp.concatenate([h_ref[j] for j in range(2 * NT)], axis=1).astype(MXU_DTYPE)
    y = _dot(lhs, wt_ref[...]) + _dot(h, wout_ref[...])
    z = _gelu_tanh(y)
    for b in range(nb):
        for t in range(T):
            z_ref[b, pl.ds(t, ncs, stride=T), :] = z[b * ncs:(b + 1) * ncs, t * LANES:(t + 1) * LANES]


def _s5_mixer(u_ssm, mats, B, L):
    T, Q, NS = SSM_CHUNK, SSM_TILES, SSM_TILE_STATE
    slab = min(SSM_SLAB, L)
    ncs = slab // T
    rows = B * ncs
    per_tile = lambda *s: pl.BlockSpec((None,) + s, lambda q, i: (q,) + (0,) * len(s), pipeline_mode=pl.Buffered(1))
    seq = pl.BlockSpec((B, slab, LANES), lambda q, i: (0, i, q))
    z = pl.pallas_call(
        functools.partial(_s5_kernel, nb=B, ncs=ncs),
        grid=(Q, L // slab),
        in_specs=[seq, per_tile(T * LANES, T * LANES), per_tile(T * LANES, 2 * NS), per_tile(2 * NS, T * LANES),
                  per_tile(1, NS), per_tile(1, NS)],
        out_specs=seq,
        out_shape=jax.ShapeDtypeStruct((B, L, SSM_WIDTH), F32),
        scratch_shapes=[pltpu.VMEM((rows, T * LANES), MXU_DTYPE), pltpu.VMEM((2 * NS // LANES, rows, LANES), F32),
                        pltpu.VMEM((2 * NS // LANES, rows, LANES), F32), pltpu.VMEM((2 * NS // LANES, B, LANES), F32)],
        compiler_params=_params("parallel", "arbitrary"),
        name="s5_mixer",
    )(u_ssm.reshape(B, L, SSM_WIDTH), *mats)
    return z.reshape(B * L, SSM_WIDTH)


def _compress_kernel(kc_ref, vc_ref, pek_ref, pev_ref, wk1_ref, wk2_ref, wv1_ref, wv2_ref, ko_ref, vo_ref, *, nrow):
    def hidden(src_ref, pe_ref, w1_ref):
        first = jnp.zeros((nrow, NSA_KV_HEADS * CMP_HIDDEN), F32)
        second = jnp.zeros((nrow, NSA_KV_HEADS * CMP_HIDDEN), F32)
        for t in range(CMP_STRIDE):
            x = src_ref[pl.ds(t, nrow, stride=CMP_STRIDE), :]
            first = first + _dot((x + pe_ref[t:t + 1, :]).astype(MXU_DTYPE), w1_ref[t])
            u = CMP_STRIDE + t
            second = second + _dot((x + pe_ref[u:u + 1, :]).astype(MXU_DTYPE), w1_ref[u])
        hid = first + pltpu.roll(second, nrow - 1, 0)
        return _gelu_tanh(hid).astype(MXU_DTYPE)

    ko_ref[...] = _dot(hidden(kc_ref, pek_ref, wk1_ref), wk2_ref[...]).astype(ko_ref.dtype)
    vo_ref[...] = _nt_dot(wv2_ref[...], hidden(vc_ref, pev_ref, wv1_ref)).astype(vo_ref.dtype)


def _compress_weights(pe, w1, w2, transposed):
    w1r = w1.reshape(CMP_BLOCK, HEAD_DIM, CMP_HIDDEN)
    eye = jnp.eye(NSA_KV_HEADS, dtype=w1.dtype)
    w1x = (w1r[:, None, :, None, :] * eye[None, :, None, :, None]).reshape(
        CMP_BLOCK, KV_WIDTH, NSA_KV_HEADS * CMP_HIDDEN)
    pe2 = jnp.concatenate([pe] * NSA_KV_HEADS, axis=1)
    w2x = (w2[None, :, None, :] * eye[:, None, :, None]).reshape(NSA_KV_HEADS * CMP_HIDDEN, KV_WIDTH)
    if transposed:
        w2x = w2x.T
    return pe2.astype(F32), w1x.astype(MXU_DTYPE), w2x.astype(MXU_DTYPE)


def _compress(kc, vc, wk, wv, B, L):
    nrow = L // CMP_STRIDE
    src = pl.BlockSpec((L, KV_WIDTH), lambda b: (b, 0))
    pek, wk1, wk2 = wk
    pev, wv1, wv2 = wv
    consts = [pek, pev, wk1, wk2, wv1, wv2]
    return pl.pallas_call(
        functools.partial(_compress_kernel, nrow=nrow),
        grid=(B,),
        in_specs=[src, src] + [_const_spec(c.shape) for c in consts],
        out_specs=[pl.BlockSpec((None, nrow, KV_WIDTH), lambda b: (b, 0, 0)),
                   pl.BlockSpec((None, KV_WIDTH, nrow), lambda b: (b, 0, 0))],
        out_shape=[jax.ShapeDtypeStruct((B, nrow, KV_WIDTH), MXU_DTYPE),
                   jax.ShapeDtypeStruct((B, KV_WIDTH, nrow), MXU_DTYPE)],
        compiler_params=_params("parallel"),
        name="kv_compress",
    )(kc, vc, *consts)


def _nsa_kernel(q_ref, gn_ref, kcmp_ref, vcmp_ref, ksx_ref, vst_ref, kw_ref, vwt_ref, ov_ref,
                o_ref, *, tq, tkb, ncmp, topk):
    q0 = pl.multiple_of(pl.program_id(1) * tq, tq)
    nsub = tq // LANES
    blocks = [(h, u) for h in range(NSA_HEADS) for u in range(nsub)]
    ncol = len(blocks) * LANES
    col = lambda h, u: slice((h * nsub + u) * LANES, (h * nsub + u + 1) * LANES)
    lo = lax.broadcasted_iota(jnp.int32, (tq, LANES), 1) < HEAD_DIM
    qt = [q_ref[:, j * LANES:(j + 1) * LANES] for j in range(NSA_GROUP)]
    qzero = jnp.zeros_like(qt[0])
    qm = [jnp.where(lo, qt[h], qzero) if h < NSA_GROUP else jnp.where(lo, qzero, qt[h - NSA_GROUP])
          for h in range(NSA_HEADS)]
    q_all = jnp.concatenate(qm, axis=0)

    def query_pos(nkeys):
        lane = lax.broadcasted_iota(jnp.int32, (nkeys, LANES), 1)
        return jnp.concatenate([q0 + u * LANES + lane for _, u in blocks], axis=1)

    def key_idx(nkeys):
        return lax.broadcasted_iota(jnp.int32, (nkeys, ncol), 0)

    cvalid = key_idx(ncmp) * CMP_STRIDE + (CMP_BLOCK - 1) <= query_pos(ncmp)
    sm = jnp.where(cvalid, _nt_dot(kcmp_ref[...], q_all), NEG)
    m = jnp.max(sm, axis=0, keepdims=True)
    e = jnp.where(cvalid, jnp.exp2(sm - m), 0.0)
    den = jnp.sum(e, axis=0, keepdims=True)
    p = e / jnp.where(den > 0.0, den, 1.0)
    o_cmp = _dot(vcmp_ref[...], p.astype(MXU_DTYPE))
    bias = {}
    j_s = lax.broadcasted_iota(jnp.int32, (MAX_SEL_BLOCKS, LANES), 0)
    lane_s = lax.broadcasted_iota(jnp.int32, (MAX_SEL_BLOCKS, LANES), 1)
    for k in range(NSA_KV_HEADS):
        for u in range(nsub):
            cur = (q0 + u * LANES + lane_s) // SEL_BLOCK
            forced = (j_s == 0) | (j_s == cur) | (j_s == cur - 1)
            psum = p[:, col(NSA_GROUP * k, u)]
            for g in range(1, NSA_GROUP):
                psum = psum + p[:, col(NSA_GROUP * k + g, u)]
            p_hi = psum.astype(MXU_DTYPE)
            p_lo = (psum - p_hi.astype(F32)).astype(MXU_DTYPE)
            imp_t = _dot(ov_ref[...], p_hi) + _dot(ov_ref[...], p_lo)
            score = jnp.where(forced, FORCE_SCORE, jnp.where(j_s <= cur, imp_t, -FORCE_SCORE))
            nblk = MAX_SEL_BLOCKS // SUBLANES
            sblk = [score[r * SUBLANES:(r + 1) * SUBLANES] for r in range(nblk)]
            jblk = j_s[0:SUBLANES]
            cnt = [jnp.zeros((SUBLANES, LANES), F32) for _ in range(nblk)]
            for i in range(MAX_SEL_BLOCKS):
                ri = jnp.broadcast_to(score[i:i + 1, :], (SUBLANES, LANES))
                for r in range(nblk):
                    if r * SUBLANES > i:
                        beats = ri >= sblk[r]
                    elif r * SUBLANES + SUBLANES - 1 < i:
                        beats = ri > sblk[r]
                    else:
                        tie = jnp.where(jblk + r * SUBLANES > i, 1.0, 0.0)
                        beats = jnp.where(ri > sblk[r], 1.0, jnp.where(ri == sblk[r], tie, 0.0)) > 0.5
                    cnt[r] = cnt[r] + jnp.where(beats, 1.0, 0.0)
            drop = jnp.concatenate([jnp.where(c < float(topk), 0.0, 1.0) for c in cnt] * 2, axis=0)
            bias[k, u] = (jnp.transpose(drop) * SEL_BIAS).astype(MXU_DTYPE)

    nfull = q0 // tkb
    bias_all = jnp.concatenate([bias[h // NSA_GROUP, u] for h, u in blocks], axis=0)
    q_sel = jnp.concatenate([q_all, bias_all], axis=1)
    t_d = query_pos(tkb)
    k_d = key_idx(tkb)

    def sel_step(kb, carry, diagonal):
        off = pl.multiple_of(kb * tkb, tkb)
        m, l, acc = carry
        s = _nt_dot(ksx_ref[pl.ds(off, tkb), :], q_sel)
        if diagonal:
            s = jnp.where(k_d + off <= t_d, s, NEG)
        m_new = jnp.maximum(m, jnp.max(s, axis=0, keepdims=True))
        alpha = jnp.exp2(m - m_new)
        pb = jnp.exp2(s - m_new)
        l = alpha * l + jnp.sum(pb, axis=0, keepdims=True)
        acc = alpha * acc + _dot(vst_ref[:, pl.ds(off, tkb)], pb.astype(MXU_DTYPE))
        return m_new, l, acc

    init = (jnp.full((1, ncol), NEG, F32), jnp.zeros((1, ncol), F32), jnp.zeros((KV_WIDTH, ncol), F32))
    carry = lax.fori_loop(0, nfull, lambda kb, c: sel_step(kb, c, False), init)
    _, l, acc = sel_step(nfull, carry, True)
    o_sel = acc / l

    span = WINDOW + tq
    start = pl.multiple_of(jnp.maximum(q0 - WINDOW, 0), LANES)
    k_w = start + key_idx(span)
    t_w = query_pos(span)
    wvalid = (k_w <= t_w) & (t_w - k_w < WINDOW)
    sw = jnp.where(wvalid, _nt_dot(kw_ref[pl.ds(start, span), :], q_all), NEG)
    ew = jnp.exp2(sw - jnp.max(sw, axis=0, keepdims=True))
    o_win = _dot(vwt_ref[:, pl.ds(start, span)], ew.astype(MXU_DTYPE)) / jnp.sum(ew, axis=0, keepdims=True)

    for u in range(nsub):
        g_t = jnp.transpose(gn_ref[u * LANES:(u + 1) * LANES, :])
        for j in range(NSA_HEADS // 2):
            halves = []
            for h in (2 * j, 2 * j + 1):
                rows = slice((h // NSA_GROUP) * HEAD_DIM, (h // NSA_GROUP + 1) * HEAD_DIM)
                gate = lambda br: g_t[3 * h + br: 3 * h + br + 1, :]
                c = col(h, u)
                halves.append(gate(0) * o_cmp[rows, c] + gate(1) * o_sel[rows, c] + gate(2) * o_win[rows, c])
            o_ref[u * LANES:(u + 1) * LANES, j * LANES:(j + 1) * LANES] = (
                jnp.transpose(jnp.concatenate(halves, axis=0)).astype(o_ref.dtype))


def _overlap_t(ncmp, n_sel):
    cs = np.arange(ncmp)[None, :] * CMP_STRIDE
    ss = np.arange(MAX_SEL_BLOCKS)[:, None] * SEL_BLOCK
    ov = np.minimum(cs + CMP_BLOCK, ss + SEL_BLOCK) - np.maximum(cs, ss)
    ov = np.maximum(ov, 0) / CMP_STRIDE
    ov[n_sel:] = 0.0
    ov[:, (ncmp - 1):] = 0.0
    return jnp.asarray(ov, dtype=MXU_DTYPE)


def _nsa_attention(q2, gn, kcmp, vcmp_t, ksx, vs_t, kw, vw_t, B, L):
    tq = Q_TILE
    tkb = min(SEL_KEY_TILE, L)
    ncmp = L // CMP_STRIDE
    n_sel = L // SEL_BLOCK
    assert n_sel <= MAX_SEL_BLOCKS and L >= WINDOW + tq and L % tkb == 0 and tkb % tq == 0
    nq = L // tq
    seq = lambda a: a.reshape(B, L, a.shape[-1])
    kv_spec = lambda w: pl.BlockSpec((None, L, w), lambda b, i: (b, 0, 0))
    vt_spec = pl.BlockSpec((None, KV_WIDTH, L), lambda b, i: (b, 0, 0))
    kc_spec = pl.BlockSpec((None, ncmp, KV_WIDTH), lambda b, i: (b, 0, 0))
    vc_spec = pl.BlockSpec((None, KV_WIDTH, ncmp), lambda b, i: (b, 0, 0))
    row = lambda w: pl.BlockSpec((tq, w), lambda b, i: (b * nq + i, 0))
    ovt = _overlap_t(ncmp, n_sel)
    return pl.pallas_call(
        functools.partial(_nsa_kernel, tq=tq, tkb=tkb, ncmp=ncmp, topk=min(SEL_TOPK, n_sel)),
        grid=(B, nq),
        in_specs=[row(NSA_WIDTH), row(LANES), kc_spec, vc_spec, kv_spec(2 * LANES), vt_spec,
                  kv_spec(LANES), vt_spec, _const_spec(ovt.shape)],
        out_specs=row(NSA_WIDTH),
        out_shape=jax.ShapeDtypeStruct((B * L, NSA_WIDTH), MXU_DTYPE),
        compiler_params=_params("parallel", "arbitrary"),
        name="nsa_attention",
    )(q2, gn, kcmp, vcmp_t, seq(ksx), vs_t, seq(kw), vw_t, ovt)


POOL_HALO = 16


def _merge_kernel(x_ref, up_ref, halo_ref, zs_ref, yn_ref, wg_ref, wpool_ref, pscale_ref, wglu_ref, bglu_ref,
                  wup_p_ref, wup_s_ref, wup_n_ref, wout_ref, g_ref, b_ref, o_ref, *, tm, nl):
    x = x_ref[...]
    xb = x.astype(MXU_DTYPE)
    first_tile = (pl.program_id(0) % nl) == 0
    halo = jnp.where(first_tile, 0.0, halo_ref[...])
    ext = jnp.concatenate([halo, up_ref[...]], axis=0)
    t_in_seq = (pl.program_id(0) % nl) * tm + lax.broadcasted_iota(jnp.int32, (tm, 1), 0)
    y_pool = []
    for gi, w in enumerate(POOL_WINDOWS):
        lanes = slice(gi * POOL_GROUP_CH, (gi + 1) * POOL_GROUP_CH)
        e = ext[:, lanes]
        span = 1
        while span < w:
            e = e + pltpu.roll(e, span, 0)
            span *= 2
        cnt = jnp.minimum(t_in_seq + 1, w).astype(F32)
        z = e[POOL_HALO:] / cnt - ext[POOL_HALO:, lanes]
        y_pool.append(_dot(z.astype(MXU_DTYPE), wpool_ref[gi]))
    y_pool = jnp.concatenate(y_pool, axis=1) * pscale_ref[...]
    zs = zs_ref[...]
    y_ssm = zs * jax.nn.sigmoid(_dot(zs.astype(MXU_DTYPE), wglu_ref[...]) + bglu_ref[...])
    d = x.shape[1]
    merged = jnp.zeros_like(x)
    for br, (y, w_ref) in enumerate(((y_pool.astype(MXU_DTYPE), wup_p_ref), (y_ssm.astype(MXU_DTYPE), wup_s_ref),
                                     (yn_ref[...], wup_n_ref))):
        gate = jax.nn.sigmoid(_dot(xb, wg_ref[:, br * d:(br + 1) * d]))
        merged = merged + gate * _dot(y, w_ref[...])
    o_ref[...] = _layer_norm(ALPHA * x + _dot(merged.astype(MXU_DTYPE), wout_ref[...]), g_ref[...], b_ref[...])


def _merge(xn, u_pool, z_ssm, y_nsa, w, L):
    n, d = xn.shape
    tm = ROW_TILE // 2
    nl = L // tm
    row = lambda wd: pl.BlockSpec((tm, wd), lambda i: (i, 0))
    halo = pl.BlockSpec((POOL_HALO, POOL_WIDTH), lambda i: (jnp.maximum(i * (tm // POOL_HALO) - 1, 0), 0))
    consts = [w["w_gate"], w["w_pool"], w["pool_scale"], w["w_glu"], w["b_glu"], w["w_up_pool"], w["w_up_ssm"],
              w["w_up_nsa"], w["w_out"], w["ln1_g"], w["ln1_b"]]
    return pl.pallas_call(
        functools.partial(_merge_kernel, tm=tm, nl=nl),
        grid=(n // tm,),
        in_specs=[row(d), row(POOL_WIDTH), halo, row(SSM_WIDTH), row(NSA_WIDTH)] + [_const_spec(c.shape) for c in consts],
        out_specs=row(d),
        out_shape=jax.ShapeDtypeStruct((n, d), F32),
        compiler_params=_params("parallel"),
        name="gated_merge",
    )(xn, u_pool, u_pool, z_ssm, y_nsa, *consts)


FF_CHUNK = 1024


def _ffn_kernel(x_ref, w1_ref, w2_ref, g_ref, b_ref, o_ref):
    x = x_ref[...]
    xb = x.astype(MXU_DTYPE)
    acc = jnp.zeros_like(x)
    for c in range(D_FF // FF_CHUNK):
        cols = slice(c * FF_CHUNK, (c + 1) * FF_CHUNK)
        h = jnp.maximum(_dot(xb, w1_ref[:, cols]), 0.0)
        acc = acc + _dot((h * h).astype(MXU_DTYPE), w2_ref[cols, :])
    o_ref[...] = _layer_norm(ALPHA * x + acc, g_ref[...], b_ref[...])


def _ffn(x1, w1, w2, g, b):
    n, d = x1.shape
    tm = ROW_TILE
    row = pl.BlockSpec((tm, d), lambda i: (i, 0))
    return pl.pallas_call(
        _ffn_kernel,
        grid=(n // tm,),
        in_specs=[row, _const_spec(w1.shape), _const_spec(w2.shape), _const_spec((1, d)), _const_spec((1, d))],
        out_specs=row,
        out_shape=jax.ShapeDtypeStruct((n, d), F32),
        compiler_params=_params("parallel"),
        name="relu2_mlp",
    )(x1, w1, w2, g.reshape(1, d), b.reshape(1, d))


def _hybrid_layer(xn, tabs, B, L, p):
    w_cat, w_vt, w_gate = _expand_in_weight(p["w_in"])
    (u_pool, u_ssm, q2, ksx, kw, kc, vs_t, vw_t, vc, gn) = _in_projection(xn, w_cat, w_vt, tabs, B, L)
    mats = _s5_matrices(p["ssm_lam_re"], p["ssm_lam_im"], p["ssm_log_dt"], p["ssm_b_re"], p["ssm_b_im"],
                        p["ssm_c_re"], p["ssm_c_im"], p["ssm_d"])
    z_ssm = _s5_mixer(u_ssm, mats, B, L)
    kcmp, vcmp_t = _compress(kc, vc, _compress_weights(p["cmp_pe_k"], p["cmp_wk1"], p["cmp_wk2"], False),
                             _compress_weights(p["cmp_pe_v"], p["cmp_wv1"], p["cmp_wv2"], True), B, L)
    y_nsa = _nsa_attention(q2, gn, kcmp, vcmp_t, ksx, vs_t, kw, vw_t, B, L)
    cast = lambda a: a.astype(MXU_DTYPE)
    row = lambda a: a.reshape(1, -1).astype(F32)
    w = dict(w_gate=w_gate, w_pool=cast(p["w_pool"]), pool_scale=row(p["pool_scale"]), w_glu=cast(p["w_glu"]),
             b_glu=row(p["b_glu"]), w_up_pool=cast(p["w_up_pool"]), w_up_ssm=cast(p["w_up_ssm"]),
             w_up_nsa=cast(p["w_up_nsa"]), w_out=cast(p["w_out"]), ln1_g=row(p["ln1_g"]), ln1_b=row(p["ln1_b"]))
    x1 = _merge(xn, u_pool, z_ssm, y_nsa, w, L)
    return _ffn(x1, cast(p["w_ff1"]), cast(p["w_ff2"]), p["ln2_g"], p["ln2_b"])


_LAYER_PARAMS = ("w_in", "w_pool", "pool_scale", "ssm_lam_re", "ssm_lam_im", "ssm_log_dt", "ssm_b_re", "ssm_b_im",
                 "ssm_c_re", "ssm_c_im", "ssm_d", "w_glu", "b_glu", "cmp_pe_k", "cmp_pe_v", "cmp_wk1", "cmp_wk2",
                 "cmp_wv1", "cmp_wv2", "w_up_pool", "w_up_ssm", "w_up_nsa", "w_out", "ln1_g", "ln1_b", "w_ff1",
                 "w_ff2", "ln2_g", "ln2_b")


def kernel(x, ln_in_g, ln_in_b, w_in, w_pool, pool_scale, ssm_lam_re, ssm_lam_im, ssm_log_dt, ssm_b_re, ssm_b_im, ssm_c_re, ssm_c_im, ssm_d, w_glu, b_glu, cmp_pe_k, cmp_pe_v, cmp_wk1, cmp_wk2, cmp_wv1, cmp_wv2, w_up_pool, w_up_ssm, w_up_nsa, w_out, ln1_g, ln1_b, w_ff1, w_ff2, ln2_g, ln2_b):
    B, L, D = x.shape
    assert D == D_MODEL and L % ROW_TILE == 0
    stacked = dict(zip(_LAYER_PARAMS, (w_in, w_pool, pool_scale, ssm_lam_re, ssm_lam_im, ssm_log_dt, ssm_b_re,
                                       ssm_b_im, ssm_c_re, ssm_c_im, ssm_d, w_glu, b_glu, cmp_pe_k, cmp_pe_v,
                                       cmp_wk1, cmp_wk2, cmp_wv1, cmp_wv2, w_up_pool, w_up_ssm, w_up_nsa, w_out,
                                       ln1_g, ln1_b, w_ff1, w_ff2, ln2_g, ln2_b)))
    tabs = _rope_tables(L)
    xn = _entry_norm(x.reshape(B * L, D), ln_in_g, ln_in_b)
    for i in range(w_in.shape[0]):
        xn = _hybrid_layer(xn, tabs, B, L, {k: v[i] for k, v in stacked.items()})
    return xn.reshape(B, L, D)
```

```python
import functools
import math

import numpy as np
import jax
import jax.numpy as jnp
from jax import lax
from jax.experimental import pallas as pl
from jax.experimental.pallas import tpu as pltpu

D_MODEL = 1024
DEPTH = 2
POOL_WIDTH = D_MODEL // 2
POOL_GROUPS = 4
POOL_WINDOWS = (2, 4, 8, 16)
POOL_GROUP_CH = POOL_WIDTH // POOL_GROUPS
SSM_WIDTH = D_MODEL // 2
SSM_GROUP_CH = 16
SSM_GROUPS = SSM_WIDTH // SSM_GROUP_CH
SSM_STATE = 64
NSA_HEADS = 8
NSA_KV_HEADS = 2
NSA_GROUP = NSA_HEADS // NSA_KV_HEADS
HEAD_DIM = 64
NSA_WIDTH = NSA_HEADS * HEAD_DIM
KV_WIDTH = NSA_KV_HEADS * HEAD_DIM
CMP_BLOCK = 32
CMP_STRIDE = 16
CMP_HIDDEN = 2 * HEAD_DIM
SEL_BLOCK = 64
SEL_TOPK = 16
WINDOW = 512
ROPE_THETA = 500000.0
ROPE_DIMS = HEAD_DIM // 4
ROPE_HALF = ROPE_DIMS // 2
NEG = -1e30
FORCE_SCORE = 1e6
D_FF = 4 * D_MODEL
ALPHA = (2 * DEPTH) ** 0.25
LN_EPS = 1e-5

LANES = 128
SUBLANES = 8
VMEM_LIMIT = 56 * 1024 * 1024

MXU_DTYPE = jnp.bfloat16
F32 = jnp.float32

LOG2E = math.log2(math.e)
SEL_BIAS = -(2.0 ** 100)
SSM_CHUNK = 8
MAX_SEL_BLOCKS = 64

ROW_TILE = 512
Q_TILE = 256
SEL_KEY_TILE = 512


def _nt_dot(a, b):
    return lax.dot_general(a, b, (((1,), (1,)), ((), ())), preferred_element_type=F32)


def _dot(a, b):
    return jnp.dot(a, b, preferred_element_type=F32)


def _gelu_tanh(x):
    return x * (0.5 * (1.0 + jnp.tanh(math.sqrt(2.0 / math.pi) * (x + 0.044715 * (x * x * x)))))


def _layer_norm(xf, g, b):
    mu = jnp.mean(xf, axis=-1, keepdims=True)
    xc = xf - mu
    var = jnp.mean(xc * xc, axis=-1, keepdims=True)
    return xc * lax.rsqrt(var + LN_EPS) * g + b


def _params(*sem):
    return pltpu.CompilerParams(dimension_semantics=sem, vmem_limit_bytes=VMEM_LIMIT)


def _const_spec(shape):
    nd = len(shape)
    return pl.BlockSpec(shape, lambda *_: (0,) * nd, pipeline_mode=pl.Buffered(1))


def _ln_kernel(x_ref, g_ref, b_ref, o_ref):
    o_ref[...] = _layer_norm(x_ref[...], g_ref[...], b_ref[...])


def _entry_norm(x2, g, b):
    n, d = x2.shape
    tm = ROW_TILE
    return pl.pallas_call(
        _ln_kernel,
        grid=(n // tm,),
        in_specs=[pl.BlockSpec((tm, d), lambda i: (i, 0)), _const_spec((1, d)), _const_spec((1, d))],
        out_specs=pl.BlockSpec((tm, d), lambda i: (i, 0)),
        out_shape=jax.ShapeDtypeStruct((n, d), F32),
        compiler_params=_params("parallel"),
        name="entry_norm",
    )(x2, g.reshape(1, d), b.reshape(1, d))


_SEC = {}
_off = 0
for _name, _w in (("pool", 512), ("ssm", 512), ("q", 512), ("ks", 128), ("kw", 128),
                  ("kc", 128), ("vc", 128), ("gn", 128)):
    _SEC[_name] = (_off, _off + _w)
    _off += _w
IN_COLS = _off


def _inproj_kernel(x_ref, w_ref, wvt_ref, cs_ref, s1_ref, s2_ref, oh_ref,
                   up_ref, us_ref, q_ref, ksx_ref, kw_ref, kc_ref, vst_ref, vwt_ref, vc_ref, gn_ref):
    xb = x_ref[...].astype(MXU_DTYPE)

    def proj(name):
        lo, hi = _SEC[name]
        return _dot(xb, w_ref[:, lo:hi])

    cs, s1, s2 = cs_ref[...], s1_ref[...], s2_ref[...]

    def rope(a):
        return a * cs + pltpu.roll(a, LANES - ROPE_HALF, 1) * s1 + pltpu.roll(a, ROPE_HALF, 1) * s2

    def roped_tiles(name):
        acc = proj(name)
        return [rope(acc[:, j * LANES:(j + 1) * LANES]) for j in range(acc.shape[1] // LANES)]

    def store_tiles(ref, tiles):
        for j, t in enumerate(tiles):
            ref[:, j * LANES:(j + 1) * LANES] = t.astype(ref.dtype)

    up_ref[...] = proj("pool")
    us_ref[...] = proj("ssm")
    store_tiles(q_ref, roped_tiles("q"))
    store_tiles(ksx_ref, roped_tiles("ks") + [oh_ref[...]])
    store_tiles(kw_ref, roped_tiles("kw"))
    store_tiles(kc_ref, roped_tiles("kc"))
    vt = _nt_dot(wvt_ref[...], xb)
    vst_ref[...] = vt[0:KV_WIDTH].astype(vst_ref.dtype)
    vwt_ref[...] = vt[KV_WIDTH:2 * KV_WIDTH].astype(vwt_ref.dtype)
    vc_ref[...] = proj("vc")
    gn_ref[...] = jax.nn.sigmoid(proj("gn"))


def _in_projection(xn, w_cat, w_vt, tabs, B, L):
    n, d = xn.shape
    tm = ROW_TILE
    nl = L // tm
    row = lambda w: pl.BlockSpec((tm, w), lambda i: (i, 0))
    tab = pl.BlockSpec((tm, LANES), lambda i: (i % nl, 0))
    tr = pl.BlockSpec((None, KV_WIDTH, tm), lambda i: (i // nl, 0, i % nl))
    bf = MXU_DTYPE
    rows = lambda w, dt: (row(w), jax.ShapeDtypeStruct((n, w), dt))
    trs = (tr, jax.ShapeDtypeStruct((B, KV_WIDTH, L), bf))
    outs = [rows(512, F32), rows(512, F32), rows(512, bf), rows(256, bf), rows(128, bf), rows(128, F32), trs, trs,
            rows(128, F32), rows(128, F32)]
    return pl.pallas_call(
        _inproj_kernel,
        grid=(n // tm,),
        in_specs=[row(d), _const_spec((d, IN_COLS)), _const_spec(w_vt.shape), tab, tab, tab, tab],
        out_specs=[s for s, _ in outs],
        out_shape=[s for _, s in outs],
        compiler_params=_params("parallel"),
        name="in_projection",
    )(xn, w_cat, w_vt, *tabs)


def _rope_tables(L):
    pos = jnp.arange(L, dtype=F32)
    inv_freq = ROPE_THETA ** (-jnp.arange(0, ROPE_DIMS, 2, dtype=F32) / ROPE_DIMS)
    ang = pos[:, None] * inv_freq[None, :]
    cos, sin = jnp.cos(ang), jnp.sin(ang)
    ones = jnp.ones((L, HEAD_DIM - ROPE_DIMS), F32)
    zeros = jnp.zeros((L, HEAD_DIM - ROPE_DIMS), F32)
    z8 = jnp.zeros((L, ROPE_HALF), F32)
    c64 = jnp.concatenate([cos, cos, ones], axis=1)
    s1_64 = jnp.concatenate([-sin, z8, zeros], axis=1)
    s2_64 = jnp.concatenate([z8, sin, zeros], axis=1)
    dup = lambda a: jnp.concatenate([a, a], axis=1)
    onehot = (jnp.arange(L)[:, None] // SEL_BLOCK == jnp.arange(MAX_SEL_BLOCKS)[None, :]).astype(F32)
    oh = jnp.concatenate([onehot, jnp.zeros((L, LANES - MAX_SEL_BLOCKS), F32)], axis=1)
    return dup(c64), dup(s1_64), dup(s2_64), oh


def _expand_in_weight(w_in):
    o = np.cumsum((0, POOL_WIDTH, SSM_WIDTH, NSA_WIDTH))
    w_pool, w_ssm = w_in[:, o[0]:o[1]], w_in[:, o[1]:o[2]]
    w_q = w_in[:, o[2]:o[3]] * (LOG2E / math.sqrt(HEAD_DIM))
    q_heads = [w_q[:, h * HEAD_DIM:(h + 1) * HEAD_DIM] for h in range(NSA_HEADS)]
    w_q = jnp.concatenate([q_heads[h] for j in range(NSA_GROUP) for h in (j, NSA_GROUP + j)], axis=1)
    kv0 = int(o[3])
    kvs = [w_in[:, kv0 + i * KV_WIDTH: kv0 + (i + 1) * KV_WIDTH] for i in range(6)]
    kc, vc, ks, vs, kw, vw = kvs
    g0 = kv0 + 6 * KV_WIDTH
    gn = w_in[:, g0:g0 + 3 * NSA_HEADS]
    gn = jnp.pad(gn, ((0, 0), (0, LANES - 3 * NSA_HEADS)))
    w_cat = jnp.concatenate([w_pool, w_ssm, w_q, ks, kw, kc, vc, gn], axis=1)
    w_vt = jnp.concatenate([vs, vw], axis=1).T
    w_gate = w_in[:, g0 + 3 * NSA_HEADS:]
    return w_cat.astype(MXU_DTYPE), w_vt.astype(MXU_DTYPE), w_gate.astype(MXU_DTYPE)


SSM_LANE_GROUPS = LANES // SSM_GROUP_CH
SSM_TILES = SSM_WIDTH // LANES
SSM_TILE_STATE = SSM_LANE_GROUPS * SSM_STATE
SSM_SLAB = 1024


def _s5_matrices(lam_re, lam_im, log_dt, b_re, b_im, c_re, c_im, d_skip):
    T, Q, GL, C, P = SSM_CHUNK, SSM_TILES, SSM_LANE_GROUPS, SSM_GROUP_CH, SSM_STATE
    hp = lax.Precision.HIGHEST
    step = jnp.exp(log_dt)[None, :, None]
    n = jnp.arange(T + 1, dtype=F32)[:, None, None]
    mag = jnp.exp(n * lam_re[None] * step)
    ang = n * lam_im[None] * step
    pr, pi = mag * jnp.cos(ang), mag * jnp.sin(ang)
    den = lam_re * lam_re + lam_im * lam_im
    n_re, n_im = pr[1] - 1.0, pi[1]
    k_re = (n_re * lam_re + n_im * lam_im) / den
    k_im = (n_im * lam_re - n_re * lam_im) / den
    bb_re = k_re[..., None] * b_re - k_im[..., None] * b_im
    bb_im = k_re[..., None] * b_im + k_im[..., None] * b_re
    ca_re = c_re[None] * pr[:, :, None, :] - c_im[None] * pi[:, :, None, :]
    ca_im = c_re[None] * pi[:, :, None, :] + c_im[None] * pr[:, :, None, :]
    taps = (jnp.einsum('tgcp,gpd->tgcd', ca_re[:T], bb_re, precision=hp)
            - jnp.einsum('tgcp,gpd->tgcd', ca_im[:T], bb_im, precision=hp))
    taps = taps.at[0].add(d_skip[:, :, None] * jnp.eye(C, dtype=F32)[None])
    tiles = lambda a: a.reshape((Q, GL) + a.shape[1:])
    taps_c = tiles(taps.transpose(1, 0, 3, 2)).transpose(0, 2, 1, 3, 4).reshape(Q, T, LANES, C)
    n_rev = (T - 1) - jnp.arange(T, dtype=F32)[:, None, None]
    mag_rev = jnp.exp(n_rev * lam_re[None] * step)
    rev_r = mag_rev * jnp.cos(n_rev * lam_im[None] * step)
    rev_i = mag_rev * jnp.sin(n_rev * lam_im[None] * step)
    bc_re = rev_r[..., None] * bb_re[None] - rev_i[..., None] * bb_im[None]
    bc_im = rev_r[..., None] * bb_im[None] + rev_i[..., None] * bb_re[None]
    to_in = lambda a: tiles(a.transpose(1, 0, 3, 2)).transpose(0, 2, 1, 3, 4).reshape(Q, T, LANES, P)
    in_c = jnp.concatenate([to_in(bc_re), to_in(bc_im)], axis=3)
    to_out = lambda a: tiles(a.transpose(1, 0, 3, 2)).transpose(0, 2, 1, 3, 4).reshape(Q, T, SSM_TILE_STATE, C)
    out_c = jnp.concatenate([to_out(ca_re[1:]), to_out(-ca_im[1:])], axis=2)
    a_r = pr[T].reshape(Q, 1, SSM_TILE_STATE)
    a_i = pi[T].reshape(Q, 1, SSM_TILE_STATE)
    cast = lambda a: a.astype(MXU_DTYPE)
    return cast(taps_c), cast(in_c), cast(out_c), a_r, a_i


def _s5_expand(taps_ref, in_ref, out_ref, wt_ref, win_ref, wout_ref):
    T, C, P, NS = SSM_CHUNK, SSM_GROUP_CH, SSM_STATE, SSM_TILE_STATE
    iota = lambda shape, ax: lax.broadcasted_iota(jnp.int32, shape, ax)
    rep_c = jnp.where(iota((C, LANES), 1) % C == iota((C, LANES), 0), 1.0, 0.0).astype(MXU_DTYPE)
    rep_p = jnp.where(iota((P, NS), 1) % P == iota((P, NS), 0), 1.0, 0.0).astype(MXU_DTYPE)
    same_cc = iota((LANES, LANES), 0) // C == iota((LANES, LANES), 1) // C
    same_cp = iota((LANES, NS), 0) // C == iota((LANES, NS), 1) // P
    same_pc = (iota((2 * NS, LANES), 0) % NS) // P == iota((2 * NS, LANES), 1) // C
    zero = jnp.zeros((LANES, LANES), MXU_DTYPE)
    lag_blocks = [jnp.where(same_cc, _dot(taps_ref[lag], rep_c), 0.0).astype(MXU_DTYPE) for lag in range(T)]
    for s in range(T):
        rows = slice(s * LANES, (s + 1) * LANES)
        for t in range(T):
            wt_ref[rows, t * LANES:(t + 1) * LANES] = lag_blocks[t - s] if t >= s else zero
        blk = in_ref[s]
        for part in range(2):
            win_ref[rows, part * NS:(part + 1) * NS] = jnp.where(
                same_cp, _dot(blk[:, part * P:(part + 1) * P], rep_p), 0.0).astype(MXU_DTYPE)
    for t in range(T):
        wout_ref[:, t * LANES:(t + 1) * LANES] = jnp.where(same_pc, _dot(out_ref[t], rep_c), 0.0).astype(MXU_DTYPE)


def _s5_kernel(u_ref, taps_ref, in_ref, out_ref, ar_ref, ai_ref, z_ref, lhs_ref, s_ref, h_ref, carry_ref,
               wt_ref, win_ref, wout_ref, *, nb, ncs):
    T, NT = SSM_CHUNK, SSM_TILE_STATE // LANES

    @pl.when(pl.program_id(1) == 0)
    def _():
        carry_ref[...] = jnp.zeros_like(carry_ref)
        _s5_expand(taps_ref, in_ref, out_ref, wt_ref, win_ref, wout_ref)

    for b in range(nb):
        for t in range(T):
            lhs_ref[b * ncs:(b + 1) * ncs, t * LANES:(t + 1) * LANES] = (
                u_ref[b, pl.ds(t, ncs, stride=T), :].astype(MXU_DTYPE))
    lhs = lhs_ref[...]
    s = _dot(lhs, win_ref[...])
    for j in range(2 * NT):
        s_ref[j] = s[:, j * LANES:(j + 1) * LANES]
    ar = [jnp.broadcast_to(ar_ref[:, j * LANES:(j + 1) * LANES], (nb, LANES)) for j in range(NT)]
    ai = [jnp.broadcast_to(ai_ref[:, j * LANES:(j + 1) * LANES], (nb, LANES)) for j in range(NT)]

    def body(c, carry):
        rows = pl.ds(c, nb, stride=ncs)
        out = []
        for j in range(NT):
            hr, hi = carry[j]
            h_ref[j, rows, :] = hr
            h_ref[NT + j, rows, :] = hi
            out.append((ar[j] * hr - ai[j] * hi + s_ref[j, rows, :],
                        ar[j] * hi + ai[j] * hr + s_ref[NT + j, rows, :]))
        return tuple(out)

    init = tuple((carry_ref[j], carry_ref[NT + j]) for j in range(NT))
    last = lax.fori_loop(0, ncs, body, init)
    for j in range(NT):
        carry_ref[j] = last[j][0]
        carry_ref[NT + j] = last[j][1]
    h = jnp.concatenate([h_ref[j] for j in range(2 * NT)], axis=1).astype(MXU_DTYPE)
    y = _dot(lhs, wt_ref[...]) + _dot(h, wout_ref[...])
    z = _gelu_tanh(y)
    for b in range(nb):
        for t in range(T):
            z_ref[b, pl.ds(t, ncs, stride=T), :] = z[b * ncs:(b + 1) * ncs, t * LANES:(t + 1) * LANES]


def _s5_mixer(u_ssm, mats, B, L):
    T, Q, NS = SSM_CHUNK, SSM_TILES, SSM_TILE_STATE
    slab = min(SSM_SLAB, L)
    ncs = slab // T
    rows = B * ncs
    per_tile = lambda *s: pl.BlockSpec((None,) + s, lambda q, i: (q,) + (0,) * len(s), pipeline_mode=pl.Buffered(1))
    seq = pl.BlockSpec((B, slab, LANES), lambda q, i: (0, i, q))
    z = pl.pallas_call(
        functools.partial(_s5_kernel, nb=B, ncs=ncs),
        grid=(Q, L // slab),
        in_specs=[seq, per_tile(T, LANES, SSM_GROUP_CH), per_tile(T, LANES, 2 * SSM_STATE),
                  per_tile(T, 2 * NS, SSM_GROUP_CH), per_tile(1, NS), per_tile(1, NS)],
        out_specs=seq,
        out_shape=jax.ShapeDtypeStruct((B, L, SSM_WIDTH), F32),
        scratch_shapes=[pltpu.VMEM((rows, T * LANES), MXU_DTYPE), pltpu.VMEM((2 * NS // LANES, rows, LANES), F32),
                        pltpu.VMEM((2 * NS // LANES, rows, LANES), F32), pltpu.VMEM((2 * NS // LANES, B, LANES), F32),
                        pltpu.VMEM((T * LANES, T * LANES), MXU_DTYPE), pltpu.VMEM((T * LANES, 2 * NS), MXU_DTYPE),
                        pltpu.VMEM((2 * NS, T * LANES), MXU_DTYPE)],
        compiler_params=_params("arbitrary", "arbitrary"),
        name="s5_mixer",
    )(u_ssm.reshape(B, L, SSM_WIDTH), *mats)
    return z.reshape(B * L, SSM_WIDTH)


def _compress_kernel(kc_ref, vc_ref, pek_ref, pev_ref, wk1_ref, wk2_ref, wv1_ref, wv2_ref, ko_ref, vo_ref, *, nrow):
    def hidden(src_ref, pe_ref, w1_ref):
        first = jnp.zeros((nrow, NSA_KV_HEADS * CMP_HIDDEN), F32)
        second = jnp.zeros((nrow, NSA_KV_HEADS * CMP_HIDDEN), F32)
        for t in range(CMP_STRIDE):
            x = src_ref[pl.ds(t, nrow, stride=CMP_STRIDE), :]
            first = first + _dot((x + pe_ref[t:t + 1, :]).astype(MXU_DTYPE), w1_ref[t])
            u = CMP_STRIDE + t
            second = second + _dot((x + pe_ref[u:u + 1, :]).astype(MXU_DTYPE), w1_ref[u])
        hid = first + pltpu.roll(second, nrow - 1, 0)
        return _gelu_tanh(hid).astype(MXU_DTYPE)

    ko_ref[...] = _dot(hidden(kc_ref, pek_ref, wk1_ref), wk2_ref[...]).astype(ko_ref.dtype)
    vo_ref[...] = _nt_dot(wv2_ref[...], hidden(vc_ref, pev_ref, wv1_ref)).astype(vo_ref.dtype)


def _compress_weights(pe, w1, w2, transposed):
    w1r = w1.reshape(CMP_BLOCK, HEAD_DIM, CMP_HIDDEN)
    eye = jnp.eye(NSA_KV_HEADS, dtype=w1.dtype)
    w1x = (w1r[:, None, :, None, :] * eye[None, :, None, :, None]).reshape(
        CMP_BLOCK, KV_WIDTH, NSA_KV_HEADS * CMP_HIDDEN)
    pe2 = jnp.concatenate([pe] * NSA_KV_HEADS, axis=1)
    w2x = (w2[None, :, None, :] * eye[:, None, :, None]).reshape(NSA_KV_HEADS * CMP_HIDDEN, KV_WIDTH)
    if transposed:
        w2x = w2x.T
    return pe2.astype(F32), w1x.astype(MXU_DTYPE), w2x.astype(MXU_DTYPE)


def _compress(kc, vc, wk, wv, B, L):
    nrow = L // CMP_STRIDE
    src = pl.BlockSpec((L, KV_WIDTH), lambda b: (b, 0))
    pek, wk1, wk2 = wk
    pev, wv1, wv2 = wv
    consts = [pek, pev, wk1, wk2, wv1, wv2]
    return pl.pallas_call(
        functools.partial(_compress_kernel, nrow=nrow),
        grid=(B,),
        in_specs=[src, src] + [_const_spec(c.shape) for c in consts],
        out_specs=[pl.BlockSpec((None, nrow, KV_WIDTH), lambda b: (b, 0, 0)),
                   pl.BlockSpec((None, KV_WIDTH, nrow), lambda b: (b, 0, 0))],
        out_shape=[jax.ShapeDtypeStruct((B, nrow, KV_WIDTH), MXU_DTYPE),
                   jax.ShapeDtypeStruct((B, KV_WIDTH, nrow), MXU_DTYPE)],
        compiler_params=_params("parallel"),
        name="kv_compress",
    )(kc, vc, *consts)


def _nsa_kernel(q_ref, gn_ref, kcmp_ref, vcmp_ref, ksx_ref, vst_ref, kw_ref, vwt_ref, ov_ref,
                o_ref, *, tq, tkb, ncmp, topk):
    q0 = pl.multiple_of(pl.program_id(1) * tq, tq)
    nsub = tq // LANES
    blocks = [(h, u) for h in range(NSA_HEADS) for u in range(nsub)]
    ncol = len(blocks) * LANES
    col = lambda h, u: slice((h * nsub + u) * LANES, (h * nsub + u + 1) * LANES)
    lo = lax.broadcasted_iota(jnp.int32, (tq, LANES), 1) < HEAD_DIM
    qt = [q_ref[:, j * LANES:(j + 1) * LANES] for j in range(NSA_GROUP)]
    qzero = jnp.zeros_like(qt[0])
    qm = [jnp.where(lo, qt[h], qzero) if h < NSA_GROUP else jnp.where(lo, qzero, qt[h - NSA_GROUP])
          for h in range(NSA_HEADS)]
    q_all = jnp.concatenate(qm, axis=0)

    def query_pos(nkeys):
        lane = lax.broadcasted_iota(jnp.int32, (nkeys, LANES), 1)
        return jnp.concatenate([q0 + u * LANES + lane for _, u in blocks], axis=1)

    def key_idx(nkeys):
        return lax.broadcasted_iota(jnp.int32, (nkeys, ncol), 0)

    cvalid = key_idx(ncmp) * CMP_STRIDE + (CMP_BLOCK - 1) <= query_pos(ncmp)
    sm = jnp.where(cvalid, _nt_dot(kcmp_ref[...], q_all), NEG)
    m = jnp.max(sm, axis=0, keepdims=True)
    e = jnp.where(cvalid, jnp.exp2(sm - m), 0.0)
    den = jnp.sum(e, axis=0, keepdims=True)
    p = e / jnp.where(den > 0.0, den, 1.0)
    o_cmp = _dot(vcmp_ref[...], p.astype(MXU_DTYPE))
    bias = {}
    j_s = lax.broadcasted_iota(jnp.int32, (MAX_SEL_BLOCKS, LANES), 0)
    lane_s = lax.broadcasted_iota(jnp.int32, (MAX_SEL_BLOCKS, LANES), 1)
    for k in range(NSA_KV_HEADS):
        for u in range(nsub):
            cur = (q0 + u * LANES + lane_s) // SEL_BLOCK
            forced = (j_s == 0) | (j_s == cur) | (j_s == cur - 1)
            psum = p[:, col(NSA_GROUP * k, u)]
            for g in range(1, NSA_GROUP):
                psum = psum + p[:, col(NSA_GROUP * k + g, u)]
            p_hi = psum.astype(MXU_DTYPE)
            p_lo = (psum - p_hi.astype(F32)).astype(MXU_DTYPE)
            imp_t = _dot(ov_ref[...], p_hi) + _dot(ov_ref[...], p_lo)
            score = jnp.where(forced, FORCE_SCORE, jnp.where(j_s <= cur, imp_t, -FORCE_SCORE))
            nblk = MAX_SEL_BLOCKS // SUBLANES
            sblk = [score[r * SUBLANES:(r + 1) * SUBLANES] for r in range(nblk)]
            jblk = j_s[0:SUBLANES]
            cnt = [jnp.zeros((SUBLANES, LANES), F32) for _ in range(nblk)]
            for i in range(MAX_SEL_BLOCKS):
                ri = jnp.broadcast_to(score[i:i + 1, :], (SUBLANES, LANES))
                for r in range(nblk):
                    if r * SUBLANES > i:
                        beats = ri >= sblk[r]
                    elif r * SUBLANES + SUBLANES - 1 < i:
                        beats = ri > sblk[r]
                    else:
                        tie = jnp.where(jblk + r * SUBLANES > i, 1.0, 0.0)
                        beats = jnp.where(ri > sblk[r], 1.0, jnp.where(ri == sblk[r], tie, 0.0)) > 0.5
                    cnt[r] = cnt[r] + jnp.where(beats, 1.0, 0.0)
            drop = jnp.concatenate([jnp.where(c < float(topk), 0.0, 1.0) for c in cnt] * 2, axis=0)
            bias[k, u] = (jnp.transpose(drop) * SEL_BIAS).astype(MXU_DTYPE)

    nfull = q0 // tkb
    bias_all = jnp.concatenate([bias[h // NSA_GROUP, u] for h, u in blocks], axis=0)
    q_sel = jnp.concatenate([q_all, bias_all], axis=1)
    t_d = query_pos(tkb)
    k_d = key_idx(tkb)

    def sel_step(kb, carry, diagonal):
        off = pl.multiple_of(kb * tkb, tkb)
        m, l, acc = carry
        s = _nt_dot(ksx_ref[pl.ds(off, tkb), :], q_sel)
        if diagonal:
            s = jnp.where(k_d + off <= t_d, s, NEG)
        m_new = jnp.maximum(m, jnp.max(s, axis=0, keepdims=True))
        alpha = jnp.exp2(m - m_new)
        pb = jnp.exp2(s - m_new)
        l = alpha * l + jnp.sum(pb, axis=0, keepdims=True)
        acc = alpha * acc + _dot(vst_ref[:, pl.ds(off, tkb)], pb.astype(MXU_DTYPE))
        return m_new, l, acc

    init = (jnp.full((1, ncol), NEG, F32), jnp.zeros((1, ncol), F32), jnp.zeros((KV_WIDTH, ncol), F32))
    carry = lax.fori_loop(0, nfull, lambda kb, c: sel_step(kb, c, False), init)
    _, l, acc = sel_step(nfull, carry, True)
    o_sel = acc / l

    span = WINDOW + tq
    start = pl.multiple_of(jnp.maximum(q0 - WINDOW, 0), LANES)
    k_w = start + key_idx(span)
    t_w = query_pos(span)
    wvalid = (k_w <= t_w) & (t_w - k_w < WINDOW)
    sw = jnp.where(wvalid, _nt_dot(kw_ref[pl.ds(start, span), :], q_all), NEG)
    ew = jnp.exp2(sw - jnp.max(sw, axis=0, keepdims=True))
    o_win = _dot(vwt_ref[:, pl.ds(start, span)], ew.astype(MXU_DTYPE)) / jnp.sum(ew, axis=0, keepdims=True)

    for u in range(nsub):
        g_t = jnp.transpose(gn_ref[u * LANES:(u + 1) * LANES, :])
        for j in range(NSA_HEADS // 2):
            halves = []
            for h in (2 * j, 2 * j + 1):
                rows = slice((h // NSA_GROUP) * HEAD_DIM, (h // NSA_GROUP + 1) * HEAD_DIM)
                gate = lambda br: g_t[3 * h + br: 3 * h + br + 1, :]
                c = col(h, u)
                halves.append(gate(0) * o_cmp[rows, c] + gate(1) * o_sel[rows, c] + gate(2) * o_win[rows, c])
            o_ref[u * LANES:(u + 1) * LANES, j * LANES:(j + 1) * LANES] = (
                jnp.transpose(jnp.concatenate(halves, axis=0)).astype(o_ref.dtype))


def _overlap_t(ncmp, n_sel):
    cs = np.arange(ncmp)[None, :] * CMP_STRIDE
    ss = np.arange(MAX_SEL_BLOCKS)[:, None] * SEL_BLOCK
    ov = np.minimum(cs + CMP_BLOCK, ss + SEL_BLOCK) - np.maximum(cs, ss)
    ov = np.maximum(ov, 0) / CMP_STRIDE
    ov[n_sel:] = 0.0
    ov[:, (ncmp - 1):] = 0.0
    return jnp.asarray(ov, dtype=MXU_DTYPE)


def _nsa_attention(q2, gn, kcmp, vcmp_t, ksx, vs_t, kw, vw_t, B, L):
    tq = Q_TILE
    tkb = min(SEL_KEY_TILE, L)
    ncmp = L // CMP_STRIDE
    n_sel = L // SEL_BLOCK
    assert n_sel <= MAX_SEL_BLOCKS and L >= WINDOW + tq and L % tkb == 0 and tkb % tq == 0
    nq = L // tq
    seq = lambda a: a.reshape(B, L, a.shape[-1])
    kv_spec = lambda w: pl.BlockSpec((None, L, w), lambda b, i: (b, 0, 0))
    vt_spec = pl.BlockSpec((None, KV_WIDTH, L), lambda b, i: (b, 0, 0))
    kc_spec = pl.BlockSpec((None, ncmp, KV_WIDTH), lambda b, i: (b, 0, 0))
    vc_spec = pl.BlockSpec((None, KV_WIDTH, ncmp), lambda b, i: (b, 0, 0))
    row = lambda w: pl.BlockSpec((tq, w), lambda b, i: (b * nq + i, 0))
    ovt = _overlap_t(ncmp, n_sel)
    return pl.pallas_call(
        functools.partial(_nsa_kernel, tq=tq, tkb=tkb, ncmp=ncmp, topk=min(SEL_TOPK, n_sel)),
        grid=(B, nq),
        in_specs=[row(NSA_WIDTH), row(LANES), kc_spec, vc_spec, kv_spec(2 * LANES), vt_spec,
                  kv_spec(LANES), vt_spec, _const_spec(ovt.shape)],
        out_specs=row(NSA_WIDTH),
        out_shape=jax.ShapeDtypeStruct((B * L, NSA_WIDTH), MXU_DTYPE),
        compiler_params=_params("parallel", "arbitrary"),
        name="nsa_attention",
    )(q2, gn, kcmp, vcmp_t, seq(ksx), vs_t, seq(kw), vw_t, ovt)


POOL_HALO = 16


def _merge_kernel(x_ref, up_ref, halo_ref, zs_ref, yn_ref, wg_ref, wpool_ref, pscale_ref, wglu_ref, bglu_ref,
                  wup_p_ref, wup_s_ref, wup_n_ref, wout_ref, g_ref, b_ref, o_ref, *, tm, nl):
    x = x_ref[...]
    xb = x.astype(MXU_DTYPE)
    first_tile = (pl.program_id(0) % nl) == 0
    halo = jnp.where(first_tile, 0.0, halo_ref[...])
    ext = jnp.concatenate([halo, up_ref[...]], axis=0)
    t_in_seq = (pl.program_id(0) % nl) * tm + lax.broadcasted_iota(jnp.int32, (tm, 1), 0)
    y_pool = []
    for gi, w in enumerate(POOL_WINDOWS):
        lanes = slice(gi * POOL_GROUP_CH, (gi + 1) * POOL_GROUP_CH)
        e = ext[:, lanes]
        span = 1
        while span < w:
            e = e + pltpu.roll(e, span, 0)
            span *= 2
        cnt = jnp.minimum(t_in_seq + 1, w).astype(F32)
        z = e[POOL_HALO:] / cnt - ext[POOL_HALO:, lanes]
        y_pool.append(_dot(z.astype(MXU_DTYPE), wpool_ref[gi]))
    y_pool = jnp.concatenate(y_pool, axis=1) * pscale_ref[...]
    zs = zs_ref[...]
    y_ssm = zs * jax.nn.sigmoid(_dot(zs.astype(MXU_DTYPE), wglu_ref[...]) + bglu_ref[...])
    d = x.shape[1]
    merged = jnp.zeros_like(x)
    for br, (y, w_ref) in enumerate(((y_pool.astype(MXU_DTYPE), wup_p_ref), (y_ssm.astype(MXU_DTYPE), wup_s_ref),
                                     (yn_ref[...], wup_n_ref))):
        gate = jax.nn.sigmoid(_dot(xb, wg_ref[:, br * d:(br + 1) * d]))
        merged = merged + gate * _dot(y, w_ref[...])
    o_ref[...] = _layer_norm(ALPHA * x + _dot(merged.astype(MXU_DTYPE), wout_ref[...]), g_ref[...], b_ref[...])


def _merge(xn, u_pool, z_ssm, y_nsa, w, L):
    n, d = xn.shape
    tm = ROW_TILE // 2
    nl = L // tm
    row = lambda wd: pl.BlockSpec((tm, wd), lambda i: (i, 0))
    halo = pl.BlockSpec((POOL_HALO, POOL_WIDTH), lambda i: (jnp.maximum(i * (tm // POOL_HALO) - 1, 0), 0))
    consts = [w["w_gate"], w["w_pool"], w["pool_scale"], w["w_glu"], w["b_glu"], w["w_up_pool"], w["w_up_ssm"],
              w["w_up_nsa"], w["w_out"], w["ln1_g"], w["ln1_b"]]
    return pl.pallas_call(
        functools.partial(_merge_kernel, tm=tm, nl=nl),
        grid=(n // tm,),
        in_specs=[row(d), row(POOL_WIDTH), halo, row(SSM_WIDTH), row(NSA_WIDTH)] + [_const_spec(c.shape) for c in consts],
        out_specs=row(d),
        out_shape=jax.ShapeDtypeStruct((n, d), F32),
        compiler_params=_params("parallel"),
        name="gated_merge",
    )(xn, u_pool, u_pool, z_ssm, y_nsa, *consts)


FF_CHUNK = 1024


def _ffn_kernel(x_ref, w1_ref, w2_ref, g_ref, b_ref, o_ref):
    x = x_ref[...]
    xb = x.astype(MXU_DTYPE)
    acc = jnp.zeros_like(x)
    for c in range(D_FF // FF_CHUNK):
        cols = slice(c * FF_CHUNK, (c + 1) * FF_CHUNK)
        h = jnp.maximum(_dot(xb, w1_ref[:, cols]), 0.0)
        acc = acc + _dot((h * h).astype(MXU_DTYPE), w2_ref[cols, :])
    o_ref[...] = _layer_norm(ALPHA * x + acc, g_ref[...], b_ref[...])


def _ffn(x1, w1, w2, g, b):
    n, d = x1.shape
    tm = ROW_TILE
    row = pl.BlockSpec((tm, d), lambda i: (i, 0))
    return pl.pallas_call(
        _ffn_kernel,
        grid=(n // tm,),
        in_specs=[row, _const_spec(w1.shape), _const_spec(w2.shape), _const_spec((1, d)), _const_spec((1, d))],
        out_specs=row,
        out_shape=jax.ShapeDtypeStruct((n, d), F32),
        compiler_params=_params("parallel"),
        name="relu2_mlp",
    )(x1, w1, w2, g.reshape(1, d), b.reshape(1, d))


def _hybrid_layer(xn, tabs, B, L, p):
    w_cat, w_vt, w_gate = _expand_in_weight(p["w_in"])
    (u_pool, u_ssm, q2, ksx, kw, kc, vs_t, vw_t, vc, gn) = _in_projection(xn, w_cat, w_vt, tabs, B, L)
    mats = _s5_matrices(p["ssm_lam_re"], p["ssm_lam_im"], p["ssm_log_dt"], p["ssm_b_re"], p["ssm_b_im"],
                        p["ssm_c_re"], p["ssm_c_im"], p["ssm_d"])
    z_ssm = _s5_mixer(u_ssm, mats, B, L)
    kcmp, vcmp_t = _compress(kc, vc, _compress_weights(p["cmp_pe_k"], p["cmp_wk1"], p["cmp_wk2"], False),
                             _compress_weights(p["cmp_pe_v"], p["cmp_wv1"], p["cmp_wv2"], True), B, L)
    y_nsa = _nsa_attention(q2, gn, kcmp, vcmp_t, ksx, vs_t, kw, vw_t, B, L)
    cast = lambda a: a.astype(MXU_DTYPE)
    row = lambda a: a.reshape(1, -1).astype(F32)
    w = dict(w_gate=w_gate, w_pool=cast(p["w_pool"]), pool_scale=row(p["pool_scale"]), w_glu=cast(p["w_glu"]),
             b_glu=row(p["b_glu"]), w_up_pool=cast(p["w_up_pool"]), w_up_ssm=cast(p["w_up_ssm"]),
             w_up_nsa=cast(p["w_up_nsa"]), w_out=cast(p["w_out"]), ln1_g=row(p["ln1_g"]), ln1_b=row(p["ln1_b"]))
    x1 = _merge(xn, u_pool, z_ssm, y_nsa, w, L)
    return _ffn(x1, cast(p["w_ff1"]), cast(p["w_ff2"]), p["ln2_g"], p["ln2_b"])


_LAYER_PARAMS = ("w_in", "w_pool", "pool_scale", "ssm_lam_re", "ssm_lam_im", "ssm_log_dt", "ssm_b_re", "ssm_b_im",
                 "ssm_c_re", "ssm_c_im", "ssm_d", "w_glu", "b_glu", "cmp_pe_k", "cmp_pe_v", "cmp_wk1", "cmp_wk2",
                 "cmp_wv1", "cmp_wv2", "w_up_pool", "w_up_ssm", "w_up_nsa", "w_out", "ln1_g", "ln1_b", "w_ff1",
                 "w_ff2", "ln2_g", "ln2_b")


def kernel(x, ln_in_g, ln_in_b, w_in, w_pool, pool_scale, ssm_lam_re, ssm_lam_im, ssm_log_dt, ssm_b_re, ssm_b_im, ssm_c_re, ssm_c_im, ssm_d, w_glu, b_glu, cmp_pe_k, cmp_pe_v, cmp_wk1, cmp_wk2, cmp_wv1, cmp_wv2, w_up_pool, w_up_ssm, w_up_nsa, w_out, ln1_g, ln1_b, w_ff1, w_ff2, ln2_g, ln2_b):
    B, L, D = x.shape
    assert D == D_MODEL and L % ROW_TILE == 0
    stacked = dict(zip(_LAYER_PARAMS, (w_in, w_pool, pool_scale, ssm_lam_re, ssm_lam_im, ssm_log_dt, ssm_b_re,
                                       ssm_b_im, ssm_c_re, ssm_c_im, ssm_d, w_glu, b_glu, cmp_pe_k, cmp_pe_v,
                                       cmp_wk1, cmp_wk2, cmp_wv1, cmp_wv2, w_up_pool, w_up_ssm, w_up_nsa, w_out,
                                       ln1_g, ln1_b, w_ff1, w_ff2, ln2_g, ln2_b)))
    tabs = _rope_tables(L)
    xn = _entry_norm(x.reshape(B * L, D), ln_in_g, ln_in_b)
    for i in range(w_in.shape[0]):
        xn = _hybrid_layer(xn, tabs, B, L, {k: v[i] for k, v in stacked.items()})
    return xn.reshape(B, L, D)
```

```python
import functools
import math

import numpy as np
import jax
import jax.numpy as jnp
from jax import lax
from jax.experimental import pallas as pl
from jax.experimental.pallas import tpu as pltpu

D_MODEL = 1024
DEPTH = 2
POOL_WIDTH = D_MODEL // 2
POOL_GROUPS = 4
POOL_WINDOWS = (2, 4, 8, 16)
POOL_GROUP_CH = POOL_WIDTH // POOL_GROUPS
SSM_WIDTH = D_MODEL // 2
SSM_GROUP_CH = 16
SSM_GROUPS = SSM_WIDTH // SSM_GROUP_CH
SSM_STATE = 64
NSA_HEADS = 8
NSA_KV_HEADS = 2
NSA_GROUP = NSA_HEADS // NSA_KV_HEADS
HEAD_DIM = 64
NSA_WIDTH = NSA_HEADS * HEAD_DIM
KV_WIDTH = NSA_KV_HEADS * HEAD_DIM
CMP_BLOCK = 32
CMP_STRIDE = 16
CMP_HIDDEN = 2 * HEAD_DIM
SEL_BLOCK = 64
SEL_TOPK = 16
WINDOW = 512
ROPE_THETA = 500000.0
ROPE_DIMS = HEAD_DIM // 4
ROPE_HALF = ROPE_DIMS // 2
NEG = -1e30
FORCE_SCORE = 1e6
D_FF = 4 * D_MODEL
ALPHA = (2 * DEPTH) ** 0.25
LN_EPS = 1e-5

LANES = 128
SUBLANES = 8
VMEM_LIMIT = 56 * 1024 * 1024

MXU_DTYPE = jnp.bfloat16
F32 = jnp.float32

LOG2E = math.log2(math.e)
SEL_BIAS = -(2.0 ** 100)
SSM_CHUNK = 8
MAX_SEL_BLOCKS = 64

ROW_TILE = 512
Q_TILE = 256
SEL_KEY_TILE = 512


def _nt_dot(a, b):
    return lax.dot_general(a, b, (((1,), (1,)), ((), ())), preferred_element_type=F32)


def _dot(a, b):
    return jnp.dot(a, b, preferred_element_type=F32)


def _gelu_tanh(x):
    return x * (0.5 * (1.0 + jnp.tanh(math.sqrt(2.0 / math.pi) * (x + 0.044715 * (x * x * x)))))


def _layer_norm(xf, g, b):
    mu = jnp.mean(xf, axis=-1, keepdims=True)
    xc = xf - mu
    var = jnp.mean(xc * xc, axis=-1, keepdims=True)
    return xc * lax.rsqrt(var + LN_EPS) * g + b


def _params(*sem):
    return pltpu.CompilerParams(dimension_semantics=sem, vmem_limit_bytes=VMEM_LIMIT)


def _const_spec(shape):
    nd = len(shape)
    return pl.BlockSpec(shape, lambda *_: (0,) * nd, pipeline_mode=pl.Buffered(1))


def _ln_kernel(x_ref, g_ref, b_ref, o_ref):
    o_ref[...] = _layer_norm(x_ref[...], g_ref[...], b_ref[...])


def _entry_norm(x2, g, b):
    n, d = x2.shape
    tm = ROW_TILE
    return pl.pallas_call(
        _ln_kernel,
        grid=(n // tm,),
        in_specs=[pl.BlockSpec((tm, d), lambda i: (i, 0)), _const_spec((1, d)), _const_spec((1, d))],
        out_specs=pl.BlockSpec((tm, d), lambda i: (i, 0)),
        out_shape=jax.ShapeDtypeStruct((n, d), F32),
        compiler_params=_params("parallel"),
        name="entry_norm",
    )(x2, g.reshape(1, d), b.reshape(1, d))


_SEC = {}
_off = 0
for _name, _w in (("pool", 512), ("ssm", 512), ("q", 512), ("ks", 128), ("kw", 128),
                  ("kc", 128), ("vc", 128), ("gn", 128)):
    _SEC[_name] = (_off, _off + _w)
    _off += _w
IN_COLS = _off


def _inproj_kernel(x_ref, w_ref, wvt_ref, cs_ref, s1_ref, s2_ref, oh_ref,
                   up_ref, us_ref, q_ref, ksx_ref, kw_ref, kc_ref, vst_ref, vwt_ref, vc_ref, gn_ref):
    xb = x_ref[...].astype(MXU_DTYPE)

    def proj(name):
        lo, hi = _SEC[name]
        return _dot(xb, w_ref[:, lo:hi])

    cs, s1, s2 = cs_ref[...], s1_ref[...], s2_ref[...]

    def rope(a):
        return a * cs + pltpu.roll(a, LANES - ROPE_HALF, 1) * s1 + pltpu.roll(a, ROPE_HALF, 1) * s2

    def roped_tiles(name):
        acc = proj(name)
        return [rope(acc[:, j * LANES:(j + 1) * LANES]) for j in range(acc.shape[1] // LANES)]

    def store_tiles(ref, tiles):
        for j, t in enumerate(tiles):
            ref[:, j * LANES:(j + 1) * LANES] = t.astype(ref.dtype)

    up_ref[...] = proj("pool")
    us_ref[...] = proj("ssm")
    store_tiles(q_ref, roped_tiles("q"))
    store_tiles(ksx_ref, roped_tiles("ks") + [oh_ref[...]])
    store_tiles(kw_ref, roped_tiles("kw"))
    store_tiles(kc_ref, roped_tiles("kc"))
    vt = _nt_dot(wvt_ref[...], xb)
    vst_ref[...] = vt[0:KV_WIDTH].astype(vst_ref.dtype)
    vwt_ref[...] = vt[KV_WIDTH:2 * KV_WIDTH].astype(vwt_ref.dtype)
    vc_ref[...] = proj("vc")
    gn_ref[...] = jax.nn.sigmoid(proj("gn"))


def _in_projection(xn, w_cat, w_vt, tabs, B, L):
    n, d = xn.shape
    tm = ROW_TILE
    nl = L // tm
    row = lambda w: pl.BlockSpec((tm, w), lambda i: (i, 0))
    tab = pl.BlockSpec((tm, LANES), lambda i: (i % nl, 0))
    tr = pl.BlockSpec((None, KV_WIDTH, tm), lambda i: (i // nl, 0, i % nl))
    bf = MXU_DTYPE
    rows = lambda w, dt: (row(w), jax.ShapeDtypeStruct((n, w), dt))
    trs = (tr, jax.ShapeDtypeStruct((B, KV_WIDTH, L), bf))
    outs = [rows(512, F32), rows(512, F32), rows(512, bf), rows(256, bf), rows(128, bf), rows(128, F32), trs, trs,
            rows(128, F32), rows(128, F32)]
    return pl.pallas_call(
        _inproj_kernel,
        grid=(n // tm,),
        in_specs=[row(d), _const_spec((d, IN_COLS)), _const_spec(w_vt.shape), tab, tab, tab, tab],
        out_specs=[s for s, _ in outs],
        out_shape=[s for _, s in outs],
        compiler_params=_params("parallel"),
        name="in_projection",
    )(xn, w_cat, w_vt, *tabs)


def _rope_tables(L):
    pos = jnp.arange(L, dtype=F32)
    inv_freq = ROPE_THETA ** (-jnp.arange(0, ROPE_DIMS, 2, dtype=F32) / ROPE_DIMS)
    ang = pos[:, None] * inv_freq[None, :]
    cos, sin = jnp.cos(ang), jnp.sin(ang)
    ones = jnp.ones((L, HEAD_DIM - ROPE_DIMS), F32)
    zeros = jnp.zeros((L, HEAD_DIM - ROPE_DIMS), F32)
    z8 = jnp.zeros((L, ROPE_HALF), F32)
    c64 = jnp.concatenate([cos, cos, ones], axis=1)
    s1_64 = jnp.concatenate([-sin, z8, zeros], axis=1)
    s2_64 = jnp.concatenate([z8, sin, zeros], axis=1)
    dup = lambda a: jnp.concatenate([a, a], axis=1)
    onehot = (jnp.arange(L)[:, None] // SEL_BLOCK == jnp.arange(MAX_SEL_BLOCKS)[None, :]).astype(F32)
    oh = jnp.concatenate([onehot, jnp.zeros((L, LANES - MAX_SEL_BLOCKS), F32)], axis=1)
    return dup(c64), dup(s1_64), dup(s2_64), oh


def _expand_in_weight(w_in):
    o = np.cumsum((0, POOL_WIDTH, SSM_WIDTH, NSA_WIDTH))
    w_pool, w_ssm = w_in[:, o[0]:o[1]], w_in[:, o[1]:o[2]]
    w_q = w_in[:, o[2]:o[3]] * (LOG2E / math.sqrt(HEAD_DIM))
    q_heads = [w_q[:, h * HEAD_DIM:(h + 1) * HEAD_DIM] for h in range(NSA_HEADS)]
    w_q = jnp.concatenate([q_heads[h] for j in range(NSA_GROUP) for h in (j, NSA_GROUP + j)], axis=1)
    kv0 = int(o[3])
    kvs = [w_in[:, kv0 + i * KV_WIDTH: kv0 + (i + 1) * KV_WIDTH] for i in range(6)]
    kc, vc, ks, vs, kw, vw = kvs
    g0 = kv0 + 6 * KV_WIDTH
    gn = w_in[:, g0:g0 + 3 * NSA_HEADS]
    gn = jnp.pad(gn, ((0, 0), (0, LANES - 3 * NSA_HEADS)))
    w_cat = jnp.concatenate([w_pool, w_ssm, w_q, ks, kw, kc, vc, gn], axis=1)
    w_vt = jnp.concatenate([vs, vw], axis=1).T
    w_gate = w_in[:, g0 + 3 * NSA_HEADS:]
    return w_cat.astype(MXU_DTYPE), w_vt.astype(MXU_DTYPE), w_gate.astype(MXU_DTYPE)


SSM_LANE_GROUPS = LANES // SSM_GROUP_CH
SSM_TILES = SSM_WIDTH // LANES
SSM_TILE_STATE = SSM_LANE_GROUPS * SSM_STATE
SSM_SLAB = 1024


def _s5_matrices(lam_re, lam_im, log_dt, b_re, b_im, c_re, c_im, d_skip):
    T, Q, GL, C, P = SSM_CHUNK, SSM_TILES, SSM_LANE_GROUPS, SSM_GROUP_CH, SSM_STATE
    hp = lax.Precision.HIGHEST
    step = jnp.exp(log_dt)[None, :, None]
    n = jnp.arange(T + 1, dtype=F32)[:, None, None]
    mag = jnp.exp(n * lam_re[None] * step)
    ang = n * lam_im[None] * step
    pr, pi = mag * jnp.cos(ang), mag * jnp.sin(ang)
    den = lam_re * lam_re + lam_im * lam_im
    n_re, n_im = pr[1] - 1.0, pi[1]
    k_re = (n_re * lam_re + n_im * lam_im) / den
    k_im = (n_im * lam_re - n_re * lam_im) / den
    bb_re = k_re[..., None] * b_re - k_im[..., None] * b_im
    bb_im = k_re[..., None] * b_im + k_im[..., None] * b_re
    ca_re = c_re[None] * pr[:, :, None, :] - c_im[None] * pi[:, :, None, :]
    ca_im = c_re[None] * pi[:, :, None, :] + c_im[None] * pr[:, :, None, :]
    taps = (jnp.einsum('tgcp,gpd->tgcd', ca_re[:T], bb_re, precision=hp)
            - jnp.einsum('tgcp,gpd->tgcd', ca_im[:T], bb_im, precision=hp))
    taps = taps.at[0].add(d_skip[:, :, None] * jnp.eye(C, dtype=F32)[None])
    tiles = lambda a: a.reshape((Q, GL) + a.shape[1:])
    taps_c = tiles(taps.transpose(1, 0, 3, 2)).transpose(0, 2, 1, 3, 4).reshape(Q, T, LANES, C)
    n_rev = (T - 1) - jnp.arange(T, dtype=F32)[:, None, None]
    mag_rev = jnp.exp(n_rev * lam_re[None] * step)
    rev_r = mag_rev * jnp.cos(n_rev * lam_im[None] * step)
    rev_i = mag_rev * jnp.sin(n_rev * lam_im[None] * step)
    bc_re = rev_r[..., None] * bb_re[None] - rev_i[..., None] * bb_im[None]
    bc_im = rev_r[..., None] * bb_im[None] + rev_i[..., None] * bb_re[None]
    to_in = lambda a: tiles(a.transpose(1, 0, 3, 2)).transpose(0, 2, 1, 3, 4).reshape(Q, T, LANES, P)
    in_c = jnp.concatenate([to_in(bc_re), to_in(bc_im)], axis=3)
    to_out = lambda a: tiles(a.transpose(1, 0, 3, 2)).transpose(0, 2, 1, 3, 4).reshape(Q, T, SSM_TILE_STATE, C)
    out_c = jnp.concatenate([to_out(ca_re[1:]), to_out(-ca_im[1:])], axis=2)
    a_r = pr[T].reshape(Q, 1, SSM_TILE_STATE)
    a_i = pi[T].reshape(Q, 1, SSM_TILE_STATE)
    cast = lambda a: a.astype(MXU_DTYPE)
    return cast(taps_c), cast(in_c), cast(out_c), a_r, a_i


def _s5_expand(taps_ref, in_ref, out_ref, wt_ref, win_ref, wout_ref):
    T, C, P, NS = SSM_CHUNK, SSM_GROUP_CH, SSM_STATE, SSM_TILE_STATE
    iota = lambda shape, ax: lax.broadcasted_iota(jnp.int32, shape, ax)
    rep_c = jnp.where(iota((C, LANES), 1) % C == iota((C, LANES), 0), 1.0, 0.0).astype(MXU_DTYPE)
    rep_p = jnp.where(iota((P, NS), 1) % P == iota((P, NS), 0), 1.0, 0.0).astype(MXU_DTYPE)
    same_cc = iota((LANES, LANES), 0) // C == iota((LANES, LANES), 1) // C
    same_cp = iota((LANES, NS), 0) // C == iota((LANES, NS), 1) // P
    same_pc = (iota((2 * NS, LANES), 0) % NS) // P == iota((2 * NS, LANES), 1) // C
    zero = jnp.zeros((LANES, LANES), MXU_DTYPE)
    lag_blocks = [jnp.where(same_cc, _dot(taps_ref[lag], rep_c), 0.0).astype(MXU_DTYPE) for lag in range(T)]
    for s in range(T):
        rows = slice(s * LANES, (s + 1) * LANES)
        for t in range(T):
            wt_ref[rows, t * LANES:(t + 1) * LANES] = lag_blocks[t - s] if t >= s else zero
        blk = in_ref[s]
        for part in range(2):
            win_ref[rows, part * NS:(part + 1) * NS] = jnp.where(
                same_cp, _dot(blk[:, part * P:(part + 1) * P], rep_p), 0.0).astype(MXU_DTYPE)
    for t in range(T):
        wout_ref[:, t * LANES:(t + 1) * LANES] = jnp.where(same_pc, _dot(out_ref[t], rep_c), 0.0).astype(MXU_DTYPE)


def _s5_kernel(u_ref, taps_ref, in_ref, out_ref, ar_ref, ai_ref, z_ref, lhs_ref, s_ref, h_ref, carry_ref,
               wt_ref, win_ref, wout_ref, zt_ref, *, nb, ncs):
    T, NT = SSM_CHUNK, SSM_TILE_STATE // LANES

    @pl.when(pl.program_id(1) == 0)
    def _():
        carry_ref[...] = jnp.zeros_like(carry_ref)
        _s5_expand(taps_ref, in_ref, out_ref, wt_ref, win_ref, wout_ref)

    for b in range(nb):
        for t in range(T):
            lhs_ref[t, pl.ds(b, ncs, stride=nb), :] = u_ref[b, pl.ds(t, ncs, stride=T), :]
    lhs = jnp.concatenate([lhs_ref[t] for t in range(T)], axis=1).astype(MXU_DTYPE)
    s = _dot(lhs, win_ref[...])
    for j in range(2 * NT):
        s_ref[j] = s[:, j * LANES:(j + 1) * LANES]
    ar = [jnp.broadcast_to(ar_ref[:, j * LANES:(j + 1) * LANES], (nb, LANES)) for j in range(NT)]
    ai = [jnp.broadcast_to(ai_ref[:, j * LANES:(j + 1) * LANES], (nb, LANES)) for j in range(NT)]

    def body(c, carry):
        rows = pl.ds(pl.multiple_of(c * nb, nb), nb)
        out = []
        for j in range(NT):
            hr, hi = carry[j]
            h_ref[j, rows, :] = hr
            h_ref[NT + j, rows, :] = hi
            out.append((ar[j] * hr - ai[j] * hi + s_ref[j, rows, :],
                        ar[j] * hi + ai[j] * hr + s_ref[NT + j, rows, :]))
        return tuple(out)

    init = tuple((carry_ref[j], carry_ref[NT + j]) for j in range(NT))
    last = lax.fori_loop(0, ncs, body, init)
    for j in range(NT):
        carry_ref[j] = last[j][0]
        carry_ref[NT + j] = last[j][1]
    h = jnp.concatenate([h_ref[j] for j in range(2 * NT)], axis=1).astype(MXU_DTYPE)
    y = _dot(lhs, wt_ref[...]) + _dot(h, wout_ref[...])
    z = _gelu_tanh(y)
    for t in range(T):
        zt_ref[t] = z[:, t * LANES:(t + 1) * LANES]
    for b in range(nb):
        for t in range(T):
            z_ref[b, pl.ds(t, ncs, stride=T), :] = zt_ref[t, pl.ds(b, ncs, stride=nb), :]


def _s5_mixer(u_ssm, mats, B, L):
    T, Q, NS = SSM_CHUNK, SSM_TILES, SSM_TILE_STATE
    slab = min(SSM_SLAB, L)
    ncs = slab // T
    rows = B * ncs
    per_tile = lambda *s: pl.BlockSpec((None,) + s, lambda q, i: (q,) + (0,) * len(s), pipeline_mode=pl.Buffered(1))
    seq = pl.BlockSpec((B, slab, LANES), lambda q, i: (0, i, q))
    z = pl.pallas_call(
        functools.partial(_s5_kernel, nb=B, ncs=ncs),
        grid=(Q, L // slab),
        in_specs=[seq, per_tile(T, LANES, SSM_GROUP_CH), per_tile(T, LANES, 2 * SSM_STATE),
                  per_tile(T, 2 * NS, SSM_GROUP_CH), per_tile(1, NS), per_tile(1, NS)],
        out_specs=seq,
        out_shape=jax.ShapeDtypeStruct((B, L, SSM_WIDTH), F32),
        scratch_shapes=[pltpu.VMEM((T, rows, LANES), F32), pltpu.VMEM((2 * NS // LANES, rows, LANES), F32),
                        pltpu.VMEM((2 * NS // LANES, rows, LANES), F32), pltpu.VMEM((2 * NS // LANES, B, LANES), F32),
                        pltpu.VMEM((T * LANES, T * LANES), MXU_DTYPE), pltpu.VMEM((T * LANES, 2 * NS), MXU_DTYPE),
                        pltpu.VMEM((2 * NS, T * LANES), MXU_DTYPE), pltpu.VMEM((T, rows, LANES), F32)],
        compiler_params=_params("arbitrary", "arbitrary"),
        name="s5_mixer",
    )(u_ssm.reshape(B, L, SSM_WIDTH), *mats)
    return z.reshape(B * L, SSM_WIDTH)


def _compress_kernel(kc_ref, vc_ref, pek_ref, pev_ref, wk1_ref, wk2_ref, wv1_ref, wv2_ref, ko_ref, vo_ref, *, nrow):
    def hidden(src_ref, pe_ref, w1_ref):
        first = jnp.zeros((nrow, NSA_KV_HEADS * CMP_HIDDEN), F32)
        second = jnp.zeros((nrow, NSA_KV_HEADS * CMP_HIDDEN), F32)
        for t in range(CMP_STRIDE):
            x = src_ref[pl.ds(t, nrow, stride=CMP_STRIDE), :]
            first = first + _dot((x + pe_ref[t:t + 1, :]).astype(MXU_DTYPE), w1_ref[t])
            u = CMP_STRIDE + t
            second = second + _dot((x + pe_ref[u:u + 1, :]).astype(MXU_DTYPE), w1_ref[u])
        hid = first + pltpu.roll(second, nrow - 1, 0)
        return _gelu_tanh(hid).astype(MXU_DTYPE)

    ko_ref[...] = _dot(hidden(kc_ref, pek_ref, wk1_ref), wk2_ref[...]).astype(ko_ref.dtype)
    vo_ref[...] = _nt_dot(wv2_ref[...], hidden(vc_ref, pev_ref, wv1_ref)).astype(vo_ref.dtype)


def _compress_weights(pe, w1, w2, transposed):
    w1r = w1.reshape(CMP_BLOCK, HEAD_DIM, CMP_HIDDEN)
    eye = jnp.eye(NSA_KV_HEADS, dtype=w1.dtype)
    w1x = (w1r[:, None, :, None, :] * eye[None, :, None, :, None]).reshape(
        CMP_BLOCK, KV_WIDTH, NSA_KV_HEADS * CMP_HIDDEN)
    pe2 = jnp.concatenate([pe] * NSA_KV_HEADS, axis=1)
    w2x = (w2[None, :, None, :] * eye[:, None, :, None]).reshape(NSA_KV_HEADS * CMP_HIDDEN, KV_WIDTH)
    if transposed:
        w2x = w2x.T
    return pe2.astype(F32), w1x.astype(MXU_DTYPE), w2x.astype(MXU_DTYPE)


def _compress(kc, vc, wk, wv, B, L):
    nrow = L // CMP_STRIDE
    src = pl.BlockSpec((L, KV_WIDTH), lambda b: (b, 0))
    pek, wk1, wk2 = wk
    pev, wv1, wv2 = wv
    consts = [pek, pev, wk1, wk2, wv1, wv2]
    return pl.pallas_call(
        functools.partial(_compress_kernel, nrow=nrow),
        grid=(B,),
        in_specs=[src, src] + [_const_spec(c.shape) for c in consts],
        out_specs=[pl.BlockSpec((None, nrow, KV_WIDTH), lambda b: (b, 0, 0)),
                   pl.BlockSpec((None, KV_WIDTH, nrow), lambda b: (b, 0, 0))],
        out_shape=[jax.ShapeDtypeStruct((B, nrow, KV_WIDTH), MXU_DTYPE),
                   jax.ShapeDtypeStruct((B, KV_WIDTH, nrow), MXU_DTYPE)],
        compiler_params=_params("parallel"),
        name="kv_compress",
    )(kc, vc, *consts)


def _nsa_kernel(q_ref, gn_ref, kcmp_ref, vcmp_ref, ksx_ref, vst_ref, kw_ref, vwt_ref, ov_ref,
                o_ref, *, tq, tkb, ncmp, topk):
    q0 = pl.multiple_of(pl.program_id(1) * tq, tq)
    nsub = tq // LANES
    blocks = [(h, u) for h in range(NSA_HEADS) for u in range(nsub)]
    ncol = len(blocks) * LANES
    col = lambda h, u: slice((h * nsub + u) * LANES, (h * nsub + u + 1) * LANES)
    lo = lax.broadcasted_iota(jnp.int32, (tq, LANES), 1) < HEAD_DIM
    qt = [q_ref[:, j * LANES:(j + 1) * LANES] for j in range(NSA_GROUP)]
    qzero = jnp.zeros_like(qt[0])
    qm = [jnp.where(lo, qt[h], qzero) if h < NSA_GROUP else jnp.where(lo, qzero, qt[h - NSA_GROUP])
          for h in range(NSA_HEADS)]
    q_all = jnp.concatenate(qm, axis=0)

    def query_pos(nkeys):
        lane = lax.broadcasted_iota(jnp.int32, (nkeys, LANES), 1)
        return jnp.concatenate([q0 + u * LANES + lane for _, u in blocks], axis=1)

    def key_idx(nkeys):
        return lax.broadcasted_iota(jnp.int32, (nkeys, ncol), 0)

    cvalid = key_idx(ncmp) * CMP_STRIDE + (CMP_BLOCK - 1) <= query_pos(ncmp)
    sm = jnp.where(cvalid, _nt_dot(kcmp_ref[...], q_all), NEG)
    m = jnp.max(sm, axis=0, keepdims=True)
    e = jnp.where(cvalid, jnp.exp2(sm - m), 0.0)
    den = jnp.sum(e, axis=0, keepdims=True)
    p = e / jnp.where(den > 0.0, den, 1.0)
    o_cmp = _dot(vcmp_ref[...], p.astype(MXU_DTYPE))
    bias = {}
    j_s = lax.broadcasted_iota(jnp.int32, (MAX_SEL_BLOCKS, LANES), 0)
    lane_s = lax.broadcasted_iota(jnp.int32, (MAX_SEL_BLOCKS, LANES), 1)
    for k in range(NSA_KV_HEADS):
        for u in range(nsub):
            cur = (q0 + u * LANES + lane_s) // SEL_BLOCK
            forced = (j_s == 0) | (j_s == cur) | (j_s == cur - 1)
            psum = p[:, col(NSA_GROUP * k, u)]
            for g in range(1, NSA_GROUP):
                psum = psum + p[:, col(NSA_GROUP * k + g, u)]
            p_hi = psum.astype(MXU_DTYPE)
            p_lo = (psum - p_hi.astype(F32)).astype(MXU_DTYPE)
            imp_t = _dot(ov_ref[...], p_hi) + _dot(ov_ref[...], p_lo)
            score = jnp.where(forced, FORCE_SCORE, jnp.where(j_s <= cur, imp_t, -FORCE_SCORE))
            nblk = MAX_SEL_BLOCKS // SUBLANES
            sblk = [score[r * SUBLANES:(r + 1) * SUBLANES] for r in range(nblk)]
            jblk = j_s[0:SUBLANES]
            cnt = [jnp.zeros((SUBLANES, LANES), F32) for _ in range(nblk)]
            for i in range(MAX_SEL_BLOCKS):
                ri = jnp.broadcast_to(score[i:i + 1, :], (SUBLANES, LANES))
                for r in range(nblk):
                    if r * SUBLANES > i:
                        beats = ri >= sblk[r]
                    elif r * SUBLANES + SUBLANES - 1 < i:
                        beats = ri > sblk[r]
                    else:
                        tie = jnp.where(jblk + r * SUBLANES > i, 1.0, 0.0)
                        beats = jnp.where(ri > sblk[r], 1.0, jnp.where(ri == sblk[r], tie, 0.0)) > 0.5
                    cnt[r] = cnt[r] + jnp.where(beats, 1.0, 0.0)
            drop = jnp.concatenate([jnp.where(c < float(topk), 0.0, 1.0) for c in cnt] * 2, axis=0)
            bias[k, u] = (jnp.transpose(drop) * SEL_BIAS).astype(MXU_DTYPE)

    nfull = q0 // tkb
    bias_all = jnp.concatenate([bias[h // NSA_GROUP, u] for h, u in blocks], axis=0)
    q_sel = jnp.concatenate([q_all, bias_all], axis=1)
    t_d = query_pos(tkb)
    k_d = key_idx(tkb)

    def sel_step(kb, carry, diagonal):
        off = pl.multiple_of(kb * tkb, tkb)
        m, l, acc = carry
        s = _nt_dot(ksx_ref[pl.ds(off, tkb), :], q_sel)
        if diagonal:
            s = jnp.where(k_d + off <= t_d, s, NEG)
        m_new = jnp.maximum(m, jnp.max(s, axis=0, keepdims=True))
        alpha = jnp.exp2(m - m_new)
        pb = jnp.exp2(s - m_new)
        l = alpha * l + jnp.sum(pb, axis=0, keepdims=True)
        acc = alpha * acc + _dot(vst_ref[:, pl.ds(off, tkb)], pb.astype(MXU_DTYPE))
        return m_new, l, acc

    init = (jnp.full((1, ncol), NEG, F32), jnp.zeros((1, ncol), F32), jnp.zeros((KV_WIDTH, ncol), F32))
    carry = lax.fori_loop(0, nfull, lambda kb, c: sel_step(kb, c, False), init)
    _, l, acc = sel_step(nfull, carry, True)
    o_sel = acc / l

    span = WINDOW + tq
    start = pl.multiple_of(jnp.maximum(q0 - WINDOW, 0), LANES)
    k_w = start + key_idx(span)
    t_w = query_pos(span)
    wvalid = (k_w <= t_w) & (t_w - k_w < WINDOW)
    sw = jnp.where(wvalid, _nt_dot(kw_ref[pl.ds(start, span), :], q_all), NEG)
    ew = jnp.exp2(sw - jnp.max(sw, axis=0, keepdims=True))
    o_win = _dot(vwt_ref[:, pl.ds(start, span)], ew.astype(MXU_DTYPE)) / jnp.sum(ew, axis=0, keepdims=True)

    for u in range(nsub):
        g_t = jnp.transpose(gn_ref[u * LANES:(u + 1) * LANES, :])
        for j in range(NSA_HEADS // 2):
            halves = []
            for h in (2 * j, 2 * j + 1):
                rows = slice((h // NSA_GROUP) * HEAD_DIM, (h // NSA_GROUP + 1) * HEAD_DIM)
                gate = lambda br: g_t[3 * h + br: 3 * h + br + 1, :]
                c = col(h, u)
                halves.append(gate(0) * o_cmp[rows, c] + gate(1) * o_sel[rows, c] + gate(2) * o_win[rows, c])
            o_ref[u * LANES:(u + 1) * LANES, j * LANES:(j + 1) * LANES] = (
                jnp.transpose(jnp.concatenate(halves, axis=0)).astype(o_ref.dtype))


def _overlap_t(ncmp, n_sel):
    cs = np.arange(ncmp)[None, :] * CMP_STRIDE
    ss = np.arange(MAX_SEL_BLOCKS)[:, None] * SEL_BLOCK
    ov = np.minimum(cs + CMP_BLOCK, ss + SEL_BLOCK) - np.maximum(cs, ss)
    ov = np.maximum(ov, 0) / CMP_STRIDE
    ov[n_sel:] = 0.0
    ov[:, (ncmp - 1):] = 0.0
    return jnp.asarray(ov, dtype=MXU_DTYPE)


def _nsa_attention(q2, gn, kcmp, vcmp_t, ksx, vs_t, kw, vw_t, B, L):
    tq = Q_TILE
    tkb = min(SEL_KEY_TILE, L)
    ncmp = L // CMP_STRIDE
    n_sel = L // SEL_BLOCK
    assert n_sel <= MAX_SEL_BLOCKS and L >= WINDOW + tq and L % tkb == 0 and tkb % tq == 0
    nq = L // tq
    seq = lambda a: a.reshape(B, L, a.shape[-1])
    kv_spec = lambda w: pl.BlockSpec((None, L, w), lambda b, i: (b, 0, 0))
    vt_spec = pl.BlockSpec((None, KV_WIDTH, L), lambda b, i: (b, 0, 0))
    kc_spec = pl.BlockSpec((None, ncmp, KV_WIDTH), lambda b, i: (b, 0, 0))
    vc_spec = pl.BlockSpec((None, KV_WIDTH, ncmp), lambda b, i: (b, 0, 0))
    row = lambda w: pl.BlockSpec((tq, w), lambda b, i: (b * nq + i, 0))
    ovt = _overlap_t(ncmp, n_sel)
    return pl.pallas_call(
        functools.partial(_nsa_kernel, tq=tq, tkb=tkb, ncmp=ncmp, topk=min(SEL_TOPK, n_sel)),
        grid=(B, nq),
        in_specs=[row(NSA_WIDTH), row(LANES), kc_spec, vc_spec, kv_spec(2 * LANES), vt_spec,
                  kv_spec(LANES), vt_spec, _const_spec(ovt.shape)],
        out_specs=row(NSA_WIDTH),
        out_shape=jax.ShapeDtypeStruct((B * L, NSA_WIDTH), MXU_DTYPE),
        compiler_params=_params("parallel", "arbitrary"),
        name="nsa_attention",
    )(q2, gn, kcmp, vcmp_t, seq(ksx), vs_t, seq(kw), vw_t, ovt)


POOL_HALO = 16


def _merge_kernel(x_ref, up_ref, halo_ref, zs_ref, yn_ref, wg_ref, wpool_ref, pscale_ref, wglu_ref, bglu_ref,
                  wup_p_ref, wup_s_ref, wup_n_ref, wout_ref, g_ref, b_ref, o_ref, *, tm, nl):
    x = x_ref[...]
    xb = x.astype(MXU_DTYPE)
    first_tile = (pl.program_id(0) % nl) == 0
    halo = jnp.where(first_tile, 0.0, halo_ref[...])
    ext = jnp.concatenate([halo, up_ref[...]], axis=0)
    t_in_seq = (pl.program_id(0) % nl) * tm + lax.broadcasted_iota(jnp.int32, (tm, 1), 0)
    y_pool = []
    for gi, w in enumerate(POOL_WINDOWS):
        lanes = slice(gi * POOL_GROUP_CH, (gi + 1) * POOL_GROUP_CH)
        e = ext[:, lanes]
        span = 1
        while span < w:
            e = e + pltpu.roll(e, span, 0)
            span *= 2
        cnt = jnp.minimum(t_in_seq + 1, w).astype(F32)
        z = e[POOL_HALO:] / cnt - ext[POOL_HALO:, lanes]
        y_pool.append(_dot(z.astype(MXU_DTYPE), wpool_ref[gi]))
    y_pool = jnp.concatenate(y_pool, axis=1) * pscale_ref[...]
    zs = zs_ref[...]
    y_ssm = zs * jax.nn.sigmoid(_dot(zs.astype(MXU_DTYPE), wglu_ref[...]) + bglu_ref[...])
    d = x.shape[1]
    merged = jnp.zeros_like(x)
    for br, (y, w_ref) in enumerate(((y_pool.astype(MXU_DTYPE), wup_p_ref), (y_ssm.astype(MXU_DTYPE), wup_s_ref),
                                     (yn_ref[...], wup_n_ref))):
        gate = jax.nn.sigmoid(_dot(xb, wg_ref[:, br * d:(br + 1) * d]))
        merged = merged + gate * _dot(y, w_ref[...])
    o_ref[...] = _layer_norm(ALPHA * x + _dot(merged.astype(MXU_DTYPE), wout_ref[...]), g_ref[...], b_ref[...])


def _merge(xn, u_pool, z_ssm, y_nsa, w, L):
    n, d = xn.shape
    tm = ROW_TILE // 2
    nl = L // tm
    row = lambda wd: pl.BlockSpec((tm, wd), lambda i: (i, 0))
    halo = pl.BlockSpec((POOL_HALO, POOL_WIDTH), lambda i: (jnp.maximum(i * (tm // POOL_HALO) - 1, 0), 0))
    consts = [w["w_gate"], w["w_pool"], w["pool_scale"], w["w_glu"], w["b_glu"], w["w_up_pool"], w["w_up_ssm"],
              w["w_up_nsa"], w["w_out"], w["ln1_g"], w["ln1_b"]]
    return pl.pallas_call(
        functools.partial(_merge_kernel, tm=tm, nl=nl),
        grid=(n // tm,),
        in_specs=[row(d), row(POOL_WIDTH), halo, row(SSM_WIDTH), row(NSA_WIDTH)] + [_const_spec(c.shape) for c in consts],
        out_specs=row(d),
        out_shape=jax.ShapeDtypeStruct((n, d), F32),
        compiler_params=_params("parallel"),
        name="gated_merge",
    )(xn, u_pool, u_pool, z_ssm, y_nsa, *consts)


FF_CHUNK = 1024


def _ffn_kernel(x_ref, w1_ref, w2_ref, g_ref, b_ref, o_ref):
    x = x_ref[...]
    xb = x.astype(MXU_DTYPE)
    acc = jnp.zeros_like(x)
    for c in range(D_FF // FF_CHUNK):
        cols = slice(c * FF_CHUNK, (c + 1) * FF_CHUNK)
        h = jnp.maximum(_dot(xb, w1_ref[:, cols]), 0.0)
        acc = acc + _dot((h * h).astype(MXU_DTYPE), w2_ref[cols, :])
    o_ref[...] = _layer_norm(ALPHA * x + acc, g_ref[...], b_ref[...])


def _ffn(x1, w1, w2, g, b):
    n, d = x1.shape
    tm = ROW_TILE
    row = pl.BlockSpec((tm, d), lambda i: (i, 0))
    return pl.pallas_call(
        _ffn_kernel,
        grid=(n // tm,),
        in_specs=[row, _const_spec(w1.shape), _const_spec(w2.shape), _const_spec((1, d)), _const_spec((1, d))],
        out_specs=row,
        out_shape=jax.ShapeDtypeStruct((n, d), F32),
        compiler_params=_params("parallel"),
        name="relu2_mlp",
    )(x1, w1, w2, g.reshape(1, d), b.reshape(1, d))


def _hybrid_layer(xn, tabs, B, L, p):
    w_cat, w_vt, w_gate = _expand_in_weight(p["w_in"])
    (u_pool, u_ssm, q2, ksx, kw, kc, vs_t, vw_t, vc, gn) = _in_projection(xn, w_cat, w_vt, tabs, B, L)
    mats = _s5_matrices(p["ssm_lam_re"], p["ssm_lam_im"], p["ssm_log_dt"], p["ssm_b_re"], p["ssm_b_im"],
                        p["ssm_c_re"], p["ssm_c_im"], p["ssm_d"])
    z_ssm = _s5_mixer(u_ssm, mats, B, L)
    kcmp, vcmp_t = _compress(kc, vc, _compress_weights(p["cmp_pe_k"], p["cmp_wk1"], p["cmp_wk2"], False),
                             _compress_weights(p["cmp_pe_v"], p["cmp_wv1"], p["cmp_wv2"], True), B, L)
    y_nsa = _nsa_attention(q2, gn, kcmp, vcmp_t, ksx, vs_t, kw, vw_t, B, L)
    cast = lambda a: a.astype(MXU_DTYPE)
    row = lambda a: a.reshape(1, -1).astype(F32)
    w = dict(w_gate=w_gate, w_pool=cast(p["w_pool"]), pool_scale=row(p["pool_scale"]), w_glu=cast(p["w_glu"]),
             b_glu=row(p["b_glu"]), w_up_pool=cast(p["w_up_pool"]), w_up_ssm=cast(p["w_up_ssm"]),
             w_up_nsa=cast(p["w_up_nsa"]), w_out=cast(p["w_out"]), ln1_g=row(p["ln1_g"]), ln1_b=row(p["ln1_b"]))
    x1 = _merge(xn, u_pool, z_ssm, y_nsa, w, L)
    return _ffn(x1, cast(p["w_ff1"]), cast(p["w_ff2"]), p["ln2_g"], p["ln2_b"])


_LAYER_PARAMS = ("w_in", "w_pool", "pool_scale", "ssm_lam_re", "ssm_lam_im", "ssm_log_dt", "ssm_b_re", "ssm_b_im",
                 "ssm_c_re", "ssm_c_im", "ssm_d", "w_glu", "b_glu", "cmp_pe_k", "cmp_pe_v", "cmp_wk1", "cmp_wk2",
                 "cmp_wv1", "cmp_wv2", "w_up_pool", "w_up_ssm", "w_up_nsa", "w_out", "ln1_g", "ln1_b", "w_ff1",
                 "w_ff2", "ln2_g", "ln2_b")


def kernel(x, ln_in_g, ln_in_b, w_in, w_pool, pool_scale, ssm_lam_re, ssm_lam_im, ssm_log_dt, ssm_b_re, ssm_b_im, ssm_c_re, ssm_c_im, ssm_d, w_glu, b_glu, cmp_pe_k, cmp_pe_v, cmp_wk1, cmp_wk2, cmp_wv1, cmp_wv2, w_up_pool, w_up_ssm, w_up_nsa, w_out, ln1_g, ln1_b, w_ff1, w_ff2, ln2_g, ln2_b):
    B, L, D = x.shape
    assert D == D_MODEL and L % ROW_TILE == 0
    stacked = dict(zip(_LAYER_PARAMS, (w_in, w_pool, pool_scale, ssm_lam_re, ssm_lam_im, ssm_log_dt, ssm_b_re,
                                       ssm_b_im, ssm_c_re, ssm_c_im, ssm_d, w_glu, b_glu, cmp_pe_k, cmp_pe_v,
                                       cmp_wk1, cmp_wk2, cmp_wv1, cmp_wv2, w_up_pool, w_up_ssm, w_up_nsa, w_out,
                                       ln1_g, ln1_b, w_ff1, w_ff2, ln2_g, ln2_b)))
    tabs = _rope_tables(L)
    xn = _entry_norm(x.reshape(B * L, D), ln_in_g, ln_in_b)
    for i in range(w_in.shape[0]):
        xn = _hybrid_layer(xn, tabs, B, L, {k: v[i] for k, v in stacked.items()})
    return xn.reshape(B, L, D)
```

```python
import functools
import math

import numpy as np
import jax
import jax.numpy as jnp
from jax import lax
from jax.experimental import pallas as pl
from jax.experimental.pallas import tpu as pltpu

D_MODEL = 1024
DEPTH = 2
POOL_WIDTH = D_MODEL // 2
POOL_GROUPS = 4
POOL_WINDOWS = (2, 4, 8, 16)
POOL_GROUP_CH = POOL_WIDTH // POOL_GROUPS
SSM_WIDTH = D_MODEL // 2
SSM_GROUP_CH = 16
SSM_GROUPS = SSM_WIDTH // SSM_GROUP_CH
SSM_STATE = 64
NSA_HEADS = 8
NSA_KV_HEADS = 2
NSA_GROUP = NSA_HEADS // NSA_KV_HEADS
HEAD_DIM = 64
NSA_WIDTH = NSA_HEADS * HEAD_DIM
KV_WIDTH = NSA_KV_HEADS * HEAD_DIM
CMP_BLOCK = 32
CMP_STRIDE = 16
CMP_HIDDEN = 2 * HEAD_DIM
SEL_BLOCK = 64
SEL_TOPK = 16
WINDOW = 512
ROPE_THETA = 500000.0
ROPE_DIMS = HEAD_DIM // 4
ROPE_HALF = ROPE_DIMS // 2
NEG = -1e30
FORCE_SCORE = 1e6
D_FF = 4 * D_MODEL
ALPHA = (2 * DEPTH) ** 0.25
LN_EPS = 1e-5

LANES = 128
SUBLANES = 8
VMEM_LIMIT = 56 * 1024 * 1024

MXU_DTYPE = jnp.bfloat16
F32 = jnp.float32

LOG2E = math.log2(math.e)
SEL_BIAS = -(2.0 ** 100)
SSM_CHUNK = 8
MAX_SEL_BLOCKS = 64

ROW_TILE = 1024
Q_TILE = 256
SEL_KEY_TILE = 512


def _nt_dot(a, b):
    return lax.dot_general(a, b, (((1,), (1,)), ((), ())), preferred_element_type=F32)


def _dot(a, b):
    return jnp.dot(a, b, preferred_element_type=F32)


def _gelu_tanh(x):
    return x * (0.5 * (1.0 + jnp.tanh(math.sqrt(2.0 / math.pi) * (x + 0.044715 * (x * x * x)))))


def _layer_norm(xf, g, b):
    mu = jnp.mean(xf, axis=-1, keepdims=True)
    xc = xf - mu
    var = jnp.mean(xc * xc, axis=-1, keepdims=True)
    return xc * lax.rsqrt(var + LN_EPS) * g + b


def _params(*sem):
    return pltpu.CompilerParams(dimension_semantics=sem, vmem_limit_bytes=VMEM_LIMIT)


def _const_spec(shape):
    nd = len(shape)
    return pl.BlockSpec(shape, lambda *_: (0,) * nd, pipeline_mode=pl.Buffered(1))


_SEC = {}
_off = 0
for _name, _w in (("pool", 512), ("ssm", 512), ("q", 512), ("ks", 128), ("kw", 128),
                  ("kc", 128), ("vc", 128), ("gn", 128)):
    _SEC[_name] = (_off, _off + _w)
    _off += _w
IN_COLS = _off


def _inproj_kernel(x_ref, *refs, pre_norm):
    x = x_ref[...]
    if pre_norm:
        x = _layer_norm(x, refs[0][...], refs[1][...])
        refs = refs[2:]
    (w_ref, wvt_ref, cs_ref, s1_ref, s2_ref, oh_ref,
     up_ref, us_ref, q_ref, ksx_ref, kw_ref, kc_ref, vst_ref, vwt_ref, vc_ref, gn_ref) = refs
    xb = x.astype(MXU_DTYPE)

    def proj(name):
        lo, hi = _SEC[name]
        return _dot(xb, w_ref[:, lo:hi])

    cs, s1, s2 = cs_ref[...], s1_ref[...], s2_ref[...]

    def rope(a):
        return a * cs + pltpu.roll(a, LANES - ROPE_HALF, 1) * s1 + pltpu.roll(a, ROPE_HALF, 1) * s2

    def roped_tiles(name):
        acc = proj(name)
        return [rope(acc[:, j * LANES:(j + 1) * LANES]) for j in range(acc.shape[1] // LANES)]

    def store_tiles(ref, tiles):
        for j, t in enumerate(tiles):
            ref[:, j * LANES:(j + 1) * LANES] = t.astype(ref.dtype)

    up_ref[...] = proj("pool")
    us_ref[...] = proj("ssm")
    store_tiles(q_ref, roped_tiles("q"))
    store_tiles(ksx_ref, roped_tiles("ks") + [oh_ref[...]])
    store_tiles(kw_ref, roped_tiles("kw"))
    store_tiles(kc_ref, roped_tiles("kc"))
    vt = _nt_dot(wvt_ref[...], xb)
    vst_ref[...] = vt[0:KV_WIDTH].astype(vst_ref.dtype)
    vwt_ref[...] = vt[KV_WIDTH:2 * KV_WIDTH].astype(vwt_ref.dtype)
    vc_ref[...] = proj("vc")
    gn_ref[...] = jax.nn.sigmoid(proj("gn"))


def _in_projection(xn, entry_ln, w_cat, w_vt, tabs, B, L):
    n, d = xn.shape
    tm = ROW_TILE
    nl = L // tm
    row = lambda w: pl.BlockSpec((tm, w), lambda i: (i, 0))
    tab = pl.BlockSpec((tm, LANES), lambda i: (i % nl, 0))
    tr = pl.BlockSpec((None, KV_WIDTH, tm), lambda i: (i // nl, 0, i % nl))
    bf = MXU_DTYPE
    rows = lambda w, dt: (row(w), jax.ShapeDtypeStruct((n, w), dt))
    trs = (tr, jax.ShapeDtypeStruct((B, KV_WIDTH, L), bf))
    outs = [rows(512, F32), rows(512, F32), rows(512, bf), rows(256, bf), rows(128, bf), rows(128, F32), trs, trs,
            rows(128, F32), rows(128, F32)]
    return pl.pallas_call(
        functools.partial(_inproj_kernel, pre_norm=bool(entry_ln)),
        grid=(n // tm,),
        in_specs=[row(d)] + [_const_spec((1, d)) for _ in entry_ln]
                 + [_const_spec((d, IN_COLS)), _const_spec(w_vt.shape), tab, tab, tab, tab],
        out_specs=[s for s, _ in outs],
        out_shape=[s for _, s in outs],
        compiler_params=_params("parallel"),
        name="in_projection",
    )(xn, *entry_ln, w_cat, w_vt, *tabs)


def _rope_tables(L):
    pos = jnp.arange(L, dtype=F32)
    inv_freq = ROPE_THETA ** (-jnp.arange(0, ROPE_DIMS, 2, dtype=F32) / ROPE_DIMS)
    ang = pos[:, None] * inv_freq[None, :]
    cos, sin = jnp.cos(ang), jnp.sin(ang)
    ones = jnp.ones((L, HEAD_DIM - ROPE_DIMS), F32)
    zeros = jnp.zeros((L, HEAD_DIM - ROPE_DIMS), F32)
    z8 = jnp.zeros((L, ROPE_HALF), F32)
    c64 = jnp.concatenate([cos, cos, ones], axis=1)
    s1_64 = jnp.concatenate([-sin, z8, zeros], axis=1)
    s2_64 = jnp.concatenate([z8, sin, zeros], axis=1)
    dup = lambda a: jnp.concatenate([a, a], axis=1)
    onehot = (jnp.arange(L)[:, None] // SEL_BLOCK == jnp.arange(MAX_SEL_BLOCKS)[None, :]).astype(F32)
    oh = jnp.concatenate([onehot, jnp.zeros((L, LANES - MAX_SEL_BLOCKS), F32)], axis=1)
    return dup(c64), dup(s1_64), dup(s2_64), oh


def _expand_in_weight(w_in):
    o = np.cumsum((0, POOL_WIDTH, SSM_WIDTH, NSA_WIDTH))
    w_pool, w_ssm = w_in[:, o[0]:o[1]], w_in[:, o[1]:o[2]]
    w_q = w_in[:, o[2]:o[3]] * (LOG2E / math.sqrt(HEAD_DIM))
    q_heads = [w_q[:, h * HEAD_DIM:(h + 1) * HEAD_DIM] for h in range(NSA_HEADS)]
    w_q = jnp.concatenate([q_heads[h] for j in range(NSA_GROUP) for h in (j, NSA_GROUP + j)], axis=1)
    kv0 = int(o[3])
    kvs = [w_in[:, kv0 + i * KV_WIDTH: kv0 + (i + 1) * KV_WIDTH] for i in range(6)]
    kc, vc, ks, vs, kw, vw = kvs
    g0 = kv0 + 6 * KV_WIDTH
    gn = w_in[:, g0:g0 + 3 * NSA_HEADS]
    gn = jnp.pad(gn, ((0, 0), (0, LANES - 3 * NSA_HEADS)))
    w_cat = jnp.concatenate([w_pool, w_ssm, w_q, ks, kw, kc, vc, gn], axis=1)
    w_vt = jnp.concatenate([vs, vw], axis=1).T
    w_gate = w_in[:, g0 + 3 * NSA_HEADS:]
    return w_cat.astype(MXU_DTYPE), w_vt.astype(MXU_DTYPE), w_gate.astype(MXU_DTYPE)


SSM_LANE_GROUPS = LANES // SSM_GROUP_CH
SSM_TILES = SSM_WIDTH // LANES
SSM_TILE_STATE = SSM_LANE_GROUPS * SSM_STATE
SSM_SLAB = 1024


def _s5_matrices(lam_re, lam_im, log_dt, b_re, b_im, c_re, c_im, d_skip):
    T, Q, GL, C, P = SSM_CHUNK, SSM_TILES, SSM_LANE_GROUPS, SSM_GROUP_CH, SSM_STATE
    hp = lax.Precision.HIGHEST
    step = jnp.exp(log_dt)[None, :, None]
    n = jnp.arange(T + 1, dtype=F32)[:, None, None]
    mag = jnp.exp(n * lam_re[None] * step)
    ang = n * lam_im[None] * step
    pr, pi = mag * jnp.cos(ang), mag * jnp.sin(ang)
    den = lam_re * lam_re + lam_im * lam_im
    n_re, n_im = pr[1] - 1.0, pi[1]
    k_re = (n_re * lam_re + n_im * lam_im) / den
    k_im = (n_im * lam_re - n_re * lam_im) / den
    bb_re = k_re[..., None] * b_re - k_im[..., None] * b_im
    bb_im = k_re[..., None] * b_im + k_im[..., None] * b_re
    ca_re = c_re[None] * pr[:, :, None, :] - c_im[None] * pi[:, :, None, :]
    ca_im = c_re[None] * pi[:, :, None, :] + c_im[None] * pr[:, :, None, :]
    taps = (jnp.einsum('tgcp,gpd->tgcd', ca_re[:T], bb_re, precision=hp)
            - jnp.einsum('tgcp,gpd->tgcd', ca_im[:T], bb_im, precision=hp))
    taps = taps.at[0].add(d_skip[:, :, None] * jnp.eye(C, dtype=F32)[None])
    tiles = lambda a: a.reshape((Q, GL) + a.shape[1:])
    taps_c = tiles(taps.transpose(1, 0, 3, 2)).transpose(0, 2, 1, 3, 4).reshape(Q, T, LANES, C)
    n_rev = (T - 1) - jnp.arange(T, dtype=F32)[:, None, None]
    mag_rev = jnp.exp(n_rev * lam_re[None] * step)
    rev_r = mag_rev * jnp.cos(n_rev * lam_im[None] * step)
    rev_i = mag_rev * jnp.sin(n_rev * lam_im[None] * step)
    bc_re = rev_r[..., None] * bb_re[None] - rev_i[..., None] * bb_im[None]
    bc_im = rev_r[..., None] * bb_im[None] + rev_i[..., None] * bb_re[None]
    to_in = lambda a: tiles(a.transpose(1, 0, 3, 2)).transpose(0, 2, 1, 3, 4).reshape(Q, T, LANES, P)
    in_c = jnp.concatenate([to_in(bc_re), to_in(bc_im)], axis=3)
    to_out = lambda a: tiles(a.transpose(1, 0, 3, 2)).transpose(0, 2, 1, 3, 4).reshape(Q, T, SSM_TILE_STATE, C)
    out_c = jnp.concatenate([to_out(ca_re[1:]), to_out(-ca_im[1:])], axis=2)
    a_r = pr[T].reshape(Q, 1, SSM_TILE_STATE)
    a_i = pi[T].reshape(Q, 1, SSM_TILE_STATE)
    cast = lambda a: a.astype(MXU_DTYPE)
    return cast(taps_c), cast(in_c), cast(out_c), a_r, a_i


def _s5_expand(taps_ref, in_ref, out_ref, wt_ref, win_ref, wout_ref):
    T, C, P, NS = SSM_CHUNK, SSM_GROUP_CH, SSM_STATE, SSM_TILE_STATE
    iota = lambda shape, ax: lax.broadcasted_iota(jnp.int32, shape, ax)
    rep_c = jnp.where(iota((C, LANES), 1) % C == iota((C, LANES), 0), 1.0, 0.0).astype(MXU_DTYPE)
    rep_p = jnp.where(iota((P, NS), 1) % P == iota((P, NS), 0), 1.0, 0.0).astype(MXU_DTYPE)
    same_cc = iota((LANES, LANES), 0) // C == iota((LANES, LANES), 1) // C
    same_cp = iota((LANES, NS), 0) // C == iota((LANES, NS), 1) // P
    same_pc = (iota((2 * NS, LANES), 0) % NS) // P == iota((2 * NS, LANES), 1) // C
    zero = jnp.zeros((LANES, LANES), MXU_DTYPE)
    lag_blocks = [jnp.where(same_cc, _dot(taps_ref[lag], rep_c), 0.0).astype(MXU_DTYPE) for lag in range(T)]
    for s in range(T):
        rows = slice(s * LANES, (s + 1) * LANES)
        for t in range(T):
            wt_ref[rows, t * LANES:(t + 1) * LANES] = lag_blocks[t - s] if t >= s else zero
        blk = in_ref[s]
        for part in range(2):
            win_ref[rows, part * NS:(part + 1) * NS] = jnp.where(
                same_cp, _dot(blk[:, part * P:(part + 1) * P], rep_p), 0.0).astype(MXU_DTYPE)
    for t in range(T):
        wout_ref[:, t * LANES:(t + 1) * LANES] = jnp.where(same_pc, _dot(out_ref[t], rep_c), 0.0).astype(MXU_DTYPE)


def _s5_kernel(u_ref, taps_ref, in_ref, out_ref, ar_ref, ai_ref, z_ref, lhs_ref, s_ref, h_ref, carry_ref,
               wt_ref, win_ref, wout_ref, zt_ref, *, nb, ncs):
    T, NT = SSM_CHUNK, SSM_TILE_STATE // LANES

    @pl.when(pl.program_id(1) == 0)
    def _():
        carry_ref[...] = jnp.zeros_like(carry_ref)
        _s5_expand(taps_ref, in_ref, out_ref, wt_ref, win_ref, wout_ref)

    for b in range(nb):
        for t in range(T):
            lhs_ref[t, pl.ds(b, ncs, stride=nb), :] = u_ref[b, pl.ds(t, ncs, stride=T), :]
    lhs = jnp.concatenate([lhs_ref[t] for t in range(T)], axis=1).astype(MXU_DTYPE)
    s = _dot(lhs, win_ref[...])
    for j in range(2 * NT):
        s_ref[j] = s[:, j * LANES:(j + 1) * LANES]
    ar = [jnp.broadcast_to(ar_ref[:, j * LANES:(j + 1) * LANES], (nb, LANES)) for j in range(NT)]
    ai = [jnp.broadcast_to(ai_ref[:, j * LANES:(j + 1) * LANES], (nb, LANES)) for j in range(NT)]

    def body(c, carry):
        rows = pl.ds(pl.multiple_of(c * nb, nb), nb)
        out = []
        for j in range(NT):
            hr, hi = carry[j]
            h_ref[j, rows, :] = hr
            h_ref[NT + j, rows, :] = hi
            out.append((ar[j] * hr - ai[j] * hi + s_ref[j, rows, :],
                        ar[j] * hi + ai[j] * hr + s_ref[NT + j, rows, :]))
        return tuple(out)

    init = tuple((carry_ref[j], carry_ref[NT + j]) for j in range(NT))
    last = lax.fori_loop(0, ncs, body, init)
    for j in range(NT):
        carry_ref[j] = last[j][0]
        carry_ref[NT + j] = last[j][1]
    h = jnp.concatenate([h_ref[j] for j in range(2 * NT)], axis=1).astype(MXU_DTYPE)
    y = _dot(lhs, wt_ref[...]) + _dot(h, wout_ref[...])
    z = _gelu_tanh(y)
    for t in range(T):
        zt_ref[t] = z[:, t * LANES:(t + 1) * LANES]
    for b in range(nb):
        for t in range(T):
            z_ref[b, pl.ds(t, ncs, stride=T), :] = zt_ref[t, pl.ds(b, ncs, stride=nb), :]


def _s5_mixer(u_ssm, mats, B, L):
    T, Q, NS = SSM_CHUNK, SSM_TILES, SSM_TILE_STATE
    slab = min(SSM_SLAB, L)
    ncs = slab // T
    rows = B * ncs
    per_tile = lambda *s: pl.BlockSpec((None,) + s, lambda q, i: (q,) + (0,) * len(s), pipeline_mode=pl.Buffered(1))
    seq = pl.BlockSpec((B, slab, LANES), lambda q, i: (0, i, q))
    z = pl.pallas_call(
        functools.partial(_s5_kernel, nb=B, ncs=ncs),
        grid=(Q, L // slab),
        in_specs=[seq, per_tile(T, LANES, SSM_GROUP_CH), per_tile(T, LANES, 2 * SSM_STATE),
                  per_tile(T, 2 * NS, SSM_GROUP_CH), per_tile(1, NS), per_tile(1, NS)],
        out_specs=seq,
        out_shape=jax.ShapeDtypeStruct((B, L, SSM_WIDTH), F32),
        scratch_shapes=[pltpu.VMEM((T, rows, LANES), F32), pltpu.VMEM((2 * NS // LANES, rows, LANES), F32),
                        pltpu.VMEM((2 * NS // LANES, rows, LANES), F32), pltpu.VMEM((2 * NS // LANES, B, LANES), F32),
                        pltpu.VMEM((T * LANES, T * LANES), MXU_DTYPE), pltpu.VMEM((T * LANES, 2 * NS), MXU_DTYPE),
                        pltpu.VMEM((2 * NS, T * LANES), MXU_DTYPE), pltpu.VMEM((T, rows, LANES), F32)],
        compiler_params=_params("arbitrary", "arbitrary"),
        name="s5_mixer",
    )(u_ssm.reshape(B, L, SSM_WIDTH), *mats)
    return z.reshape(B * L, SSM_WIDTH)


def _compress_kernel(kc_ref, vc_ref, pek_ref, pev_ref, wk1_ref, wk2_ref, wv1_ref, wv2_ref, ko_ref, vo_ref, *, nrow):
    def hidden(src_ref, pe_ref, w1_ref):
        first = jnp.zeros((nrow, NSA_KV_HEADS * CMP_HIDDEN), F32)
        second = jnp.zeros((nrow, NSA_KV_HEADS * CMP_HIDDEN), F32)
        for t in range(CMP_STRIDE):
            x = src_ref[pl.ds(t, nrow, stride=CMP_STRIDE), :]
            first = first + _dot((x + pe_ref[t:t + 1, :]).astype(MXU_DTYPE), w1_ref[t])
            u = CMP_STRIDE + t
            second = second + _dot((x + pe_ref[u:u + 1, :]).astype(MXU_DTYPE), w1_ref[u])
        hid = first + pltpu.roll(second, nrow - 1, 0)
        return _gelu_tanh(hid).astype(MXU_DTYPE)

    ko_ref[...] = _dot(hidden(kc_ref, pek_ref, wk1_ref), wk2_ref[...]).astype(ko_ref.dtype)
    vo_ref[...] = _nt_dot(wv2_ref[...], hidden(vc_ref, pev_ref, wv1_ref)).astype(vo_ref.dtype)


def _compress_weights(pe, w1, w2, transposed):
    w1r = w1.reshape(CMP_BLOCK, HEAD_DIM, CMP_HIDDEN)
    eye = jnp.eye(NSA_KV_HEADS, dtype=w1.dtype)
    w1x = (w1r[:, None, :, None, :] * eye[None, :, None, :, None]).reshape(
        CMP_BLOCK, KV_WIDTH, NSA_KV_HEADS * CMP_HIDDEN)
    pe2 = jnp.concatenate([pe] * NSA_KV_HEADS, axis=1)
    w2x = (w2[None, :, None, :] * eye[:, None, :, None]).reshape(NSA_KV_HEADS * CMP_HIDDEN, KV_WIDTH)
    if transposed:
        w2x = w2x.T
    return pe2.astype(F32), w1x.astype(MXU_DTYPE), w2x.astype(MXU_DTYPE)


def _compress(kc, vc, wk, wv, B, L):
    nrow = L // CMP_STRIDE
    src = pl.BlockSpec((L, KV_WIDTH), lambda b: (b, 0))
    pek, wk1, wk2 = wk
    pev, wv1, wv2 = wv
    consts = [pek, pev, wk1, wk2, wv1, wv2]
    return pl.pallas_call(
        functools.partial(_compress_kernel, nrow=nrow),
        grid=(B,),
        in_specs=[src, src] + [_const_spec(c.shape) for c in consts],
        out_specs=[pl.BlockSpec((None, nrow, KV_WIDTH), lambda b: (b, 0, 0)),
                   pl.BlockSpec((None, KV_WIDTH, nrow), lambda b: (b, 0, 0))],
        out_shape=[jax.ShapeDtypeStruct((B, nrow, KV_WIDTH), MXU_DTYPE),
                   jax.ShapeDtypeStruct((B, KV_WIDTH, nrow), MXU_DTYPE)],
        compiler_params=_params("parallel"),
        name="kv_compress",
    )(kc, vc, *consts)


def _nsa_kernel(q_ref, gn_ref, kcmp_ref, vcmp_ref, ksx_ref, vst_ref, kw_ref, vwt_ref, ov_ref,
                o_ref, *, tq, tkb, ncmp, topk):
    q0 = pl.multiple_of(pl.program_id(1) * tq, tq)
    nsub = tq // LANES
    blocks = [(h, u) for h in range(NSA_HEADS) for u in range(nsub)]
    ncol = len(blocks) * LANES
    col = lambda h, u: slice((h * nsub + u) * LANES, (h * nsub + u + 1) * LANES)
    lo = lax.broadcasted_iota(jnp.int32, (tq, LANES), 1) < HEAD_DIM
    qt = [q_ref[:, j * LANES:(j + 1) * LANES] for j in range(NSA_GROUP)]
    qzero = jnp.zeros_like(qt[0])
    qm = [jnp.where(lo, qt[h], qzero) if h < NSA_GROUP else jnp.where(lo, qzero, qt[h - NSA_GROUP])
          for h in range(NSA_HEADS)]
    q_all = jnp.concatenate(qm, axis=0)

    def query_pos(nkeys):
        lane = lax.broadcasted_iota(jnp.int32, (nkeys, LANES), 1)
        return jnp.concatenate([q0 + u * LANES + lane for _, u in blocks], axis=1)

    def key_idx(nkeys):
        return lax.broadcasted_iota(jnp.int32, (nkeys, ncol), 0)

    cvalid = key_idx(ncmp) * CMP_STRIDE + (CMP_BLOCK - 1) <= query_pos(ncmp)
    sm = jnp.where(cvalid, _nt_dot(kcmp_ref[...], q_all), NEG)
    m = jnp.max(sm, axis=0, keepdims=True)
    e = jnp.where(cvalid, jnp.exp2(sm - m), 0.0)
    den = jnp.sum(e, axis=0, keepdims=True)
    p = e / jnp.where(den > 0.0, den, 1.0)
    o_cmp = _dot(vcmp_ref[...], p.astype(MXU_DTYPE))
    bias = {}
    j_s = lax.broadcasted_iota(jnp.int32, (MAX_SEL_BLOCKS, LANES), 0)
    lane_s = lax.broadcasted_iota(jnp.int32, (MAX_SEL_BLOCKS, LANES), 1)
    for k in range(NSA_KV_HEADS):
        for u in range(nsub):
            cur = (q0 + u * LANES + lane_s) // SEL_BLOCK
            forced = (j_s == 0) | (j_s == cur) | (j_s == cur - 1)
            psum = p[:, col(NSA_GROUP * k, u)]
            for g in range(1, NSA_GROUP):
                psum = psum + p[:, col(NSA_GROUP * k + g, u)]
            p_hi = psum.astype(MXU_DTYPE)
            p_lo = (psum - p_hi.astype(F32)).astype(MXU_DTYPE)
            imp_t = _dot(ov_ref[...], p_hi) + _dot(ov_ref[...], p_lo)
            score = jnp.where(forced, FORCE_SCORE, jnp.where(j_s <= cur, imp_t, -FORCE_SCORE))
            nblk = MAX_SEL_BLOCKS // SUBLANES
            sblk = [score[r * SUBLANES:(r + 1) * SUBLANES] for r in range(nblk)]
            jblk = j_s[0:SUBLANES]
            cnt = [jnp.zeros((SUBLANES, LANES), F32) for _ in range(nblk)]
            for i in range(MAX_SEL_BLOCKS):
                ri = jnp.broadcast_to(score[i:i + 1, :], (SUBLANES, LANES))
                for r in range(nblk):
                    if r * SUBLANES > i:
                        beats = ri >= sblk[r]
                    elif r * SUBLANES + SUBLANES - 1 < i:
                        beats = ri > sblk[r]
                    else:
                        tie = jnp.where(jblk + r * SUBLANES > i, 1.0, 0.0)
                        beats = jnp.where(ri > sblk[r], 1.0, jnp.where(ri == sblk[r], tie, 0.0)) > 0.5
                    cnt[r] = cnt[r] + jnp.where(beats, 1.0, 0.0)
            drop = jnp.concatenate([jnp.where(c < float(topk), 0.0, 1.0) for c in cnt] * 2, axis=0)
            bias[k, u] = (jnp.transpose(drop) * SEL_BIAS).astype(MXU_DTYPE)

    nfull = q0 // tkb
    bias_all = jnp.concatenate([bias[h // NSA_GROUP, u] for h, u in blocks], axis=0)
    q_sel = jnp.concatenate([q_all, bias_all], axis=1)
    t_d = query_pos(tkb)
    k_d = key_idx(tkb)

    def sel_step(kb, carry, diagonal):
        off = pl.multiple_of(kb * tkb, tkb)
        m, l, acc = carry
        s = _nt_dot(ksx_ref[pl.ds(off, tkb), :], q_sel)
        if diagonal:
            s = jnp.where(k_d + off <= t_d, s, NEG)
        m_new = jnp.maximum(m, jnp.max(s, axis=0, keepdims=True))
        alpha = jnp.exp2(m - m_new)
        pb = jnp.exp2(s - m_new)
        l = alpha * l + jnp.sum(pb, axis=0, keepdims=True)
        acc = alpha * acc + _dot(vst_ref[:, pl.ds(off, tkb)], pb.astype(MXU_DTYPE))
        return m_new, l, acc

    init = (jnp.full((1, ncol), NEG, F32), jnp.zeros((1, ncol), F32), jnp.zeros((KV_WIDTH, ncol), F32))
    carry = lax.fori_loop(0, nfull, lambda kb, c: sel_step(kb, c, False), init)
    _, l, acc = sel_step(nfull, carry, True)
    o_sel = acc / l

    span = WINDOW + tq
    start = pl.multiple_of(jnp.maximum(q0 - WINDOW, 0), LANES)
    k_w = start + key_idx(span)
    t_w = query_pos(span)
    wvalid = (k_w <= t_w) & (t_w - k_w < WINDOW)
    sw = jnp.where(wvalid, _nt_dot(kw_ref[pl.ds(start, span), :], q_all), NEG)
    ew = jnp.exp2(sw - jnp.max(sw, axis=0, keepdims=True))
    o_win = _dot(vwt_ref[:, pl.ds(start, span)], ew.astype(MXU_DTYPE)) / jnp.sum(ew, axis=0, keepdims=True)

    for u in range(nsub):
        g_t = jnp.transpose(gn_ref[u * LANES:(u + 1) * LANES, :])
        for j in range(NSA_HEADS // 2):
            halves = []
            for h in (2 * j, 2 * j + 1):
                rows = slice((h // NSA_GROUP) * HEAD_DIM, (h // NSA_GROUP + 1) * HEAD_DIM)
                gate = lambda br: g_t[3 * h + br: 3 * h + br + 1, :]
                c = col(h, u)
                halves.append(gate(0) * o_cmp[rows, c] + gate(1) * o_sel[rows, c] + gate(2) * o_win[rows, c])
            o_ref[u * LANES:(u + 1) * LANES, j * LANES:(j + 1) * LANES] = (
                jnp.transpose(jnp.concatenate(halves, axis=0)).astype(o_ref.dtype))


def _overlap_t(ncmp, n_sel):
    cs = np.arange(ncmp)[None, :] * CMP_STRIDE
    ss = np.arange(MAX_SEL_BLOCKS)[:, None] * SEL_BLOCK
    ov = np.minimum(cs + CMP_BLOCK, ss + SEL_BLOCK) - np.maximum(cs, ss)
    ov = np.maximum(ov, 0) / CMP_STRIDE
    ov[n_sel:] = 0.0
    ov[:, (ncmp - 1):] = 0.0
    return jnp.asarray(ov, dtype=MXU_DTYPE)


def _nsa_attention(q2, gn, kcmp, vcmp_t, ksx, vs_t, kw, vw_t, B, L):
    tq = Q_TILE
    tkb = min(SEL_KEY_TILE, L)
    ncmp = L // CMP_STRIDE
    n_sel = L // SEL_BLOCK
    assert n_sel <= MAX_SEL_BLOCKS and L >= WINDOW + tq and L % tkb == 0 and tkb % tq == 0
    nq = L // tq
    seq = lambda a: a.reshape(B, L, a.shape[-1])
    kv_spec = lambda w: pl.BlockSpec((None, L, w), lambda b, i: (b, 0, 0))
    vt_spec = pl.BlockSpec((None, KV_WIDTH, L), lambda b, i: (b, 0, 0))
    kc_spec = pl.BlockSpec((None, ncmp, KV_WIDTH), lambda b, i: (b, 0, 0))
    vc_spec = pl.BlockSpec((None, KV_WIDTH, ncmp), lambda b, i: (b, 0, 0))
    row = lambda w: pl.BlockSpec((tq, w), lambda b, i: (b * nq + i, 0))
    ovt = _overlap_t(ncmp, n_sel)
    return pl.pallas_call(
        functools.partial(_nsa_kernel, tq=tq, tkb=tkb, ncmp=ncmp, topk=min(SEL_TOPK, n_sel)),
        grid=(B, nq),
        in_specs=[row(NSA_WIDTH), row(LANES), kc_spec, vc_spec, kv_spec(2 * LANES), vt_spec,
                  kv_spec(LANES), vt_spec, _const_spec(ovt.shape)],
        out_specs=row(NSA_WIDTH),
        out_shape=jax.ShapeDtypeStruct((B * L, NSA_WIDTH), MXU_DTYPE),
        compiler_params=_params("parallel", "arbitrary"),
        name="nsa_attention",
    )(q2, gn, kcmp, vcmp_t, seq(ksx), vs_t, seq(kw), vw_t, ovt)


POOL_HALO = 16


def _merge_kernel(x_ref, *refs, tm, nl, pre_norm):
    x = x_ref[...]
    if pre_norm:
        x = _layer_norm(x, refs[0][...], refs[1][...])
        refs = refs[2:]
    (up_ref, halo_ref, zs_ref, yn_ref, wg_ref, wpool_ref, pscale_ref, wglu_ref, bglu_ref,
     wup_p_ref, wup_s_ref, wup_n_ref, wout_ref, g_ref, b_ref, o_ref) = refs
    xb = x.astype(MXU_DTYPE)
    first_tile = (pl.program_id(0) % nl) == 0
    halo = jnp.where(first_tile, 0.0, halo_ref[...])
    ext = jnp.concatenate([halo, up_ref[...]], axis=0)
    t_in_seq = (pl.program_id(0) % nl) * tm + lax.broadcasted_iota(jnp.int32, (tm, 1), 0)
    y_pool = []
    for gi, w in enumerate(POOL_WINDOWS):
        lanes = slice(gi * POOL_GROUP_CH, (gi + 1) * POOL_GROUP_CH)
        e = ext[:, lanes]
        span = 1
        while span < w:
            e = e + pltpu.roll(e, span, 0)
            span *= 2
        cnt = jnp.minimum(t_in_seq + 1, w).astype(F32)
        z = e[POOL_HALO:] / cnt - ext[POOL_HALO:, lanes]
        y_pool.append(_dot(z.astype(MXU_DTYPE), wpool_ref[gi]))
    y_pool = jnp.concatenate(y_pool, axis=1) * pscale_ref[...]
    zs = zs_ref[...]
    y_ssm = zs * jax.nn.sigmoid(_dot(zs.astype(MXU_DTYPE), wglu_ref[...]) + bglu_ref[...])
    d = x.shape[1]
    merged = jnp.zeros_like(x)
    for br, (y, w_ref) in enumerate(((y_pool.astype(MXU_DTYPE), wup_p_ref), (y_ssm.astype(MXU_DTYPE), wup_s_ref),
                                     (yn_ref[...], wup_n_ref))):
        gate = jax.nn.sigmoid(_dot(xb, wg_ref[:, br * d:(br + 1) * d]))
        merged = merged + gate * _dot(y, w_ref[...])
    o_ref[...] = _layer_norm(ALPHA * x + _dot(merged.astype(MXU_DTYPE), wout_ref[...]), g_ref[...], b_ref[...])


def _merge(xn, entry_ln, u_pool, z_ssm, y_nsa, w, L):
    n, d = xn.shape
    tm = ROW_TILE // 2
    nl = L // tm
    row = lambda wd: pl.BlockSpec((tm, wd), lambda i: (i, 0))
    halo = pl.BlockSpec((POOL_HALO, POOL_WIDTH), lambda i: (jnp.maximum(i * (tm // POOL_HALO) - 1, 0), 0))
    consts = [w["w_gate"], w["w_pool"], w["pool_scale"], w["w_glu"], w["b_glu"], w["w_up_pool"], w["w_up_ssm"],
              w["w_up_nsa"], w["w_out"], w["ln1_g"], w["ln1_b"]]
    return pl.pallas_call(
        functools.partial(_merge_kernel, tm=tm, nl=nl, pre_norm=bool(entry_ln)),
        grid=(n // tm,),
        in_specs=[row(d)] + [_const_spec((1, d)) for _ in entry_ln]
                 + [row(POOL_WIDTH), halo, row(SSM_WIDTH), row(NSA_WIDTH)] + [_const_spec(c.shape) for c in consts],
        out_specs=row(d),
        out_shape=jax.ShapeDtypeStruct((n, d), F32),
        compiler_params=_params("parallel"),
        name="gated_merge",
    )(xn, *entry_ln, u_pool, u_pool, z_ssm, y_nsa, *consts)


FF_CHUNK = 1024


def _ffn_kernel(x_ref, w1_ref, w2_ref, g_ref, b_ref, o_ref):
    x = x_ref[...]
    xb = x.astype(MXU_DTYPE)
    acc = jnp.zeros_like(x)
    for c in range(D_FF // FF_CHUNK):
        cols = slice(c * FF_CHUNK, (c + 1) * FF_CHUNK)
        h = jnp.maximum(_dot(xb, w1_ref[:, cols]), 0.0)
        acc = acc + _dot((h * h).astype(MXU_DTYPE), w2_ref[cols, :])
    o_ref[...] = _layer_norm(ALPHA * x + acc, g_ref[...], b_ref[...])


def _ffn(x1, w1, w2, g, b):
    n, d = x1.shape
    tm = ROW_TILE
    row = pl.BlockSpec((tm, d), lambda i: (i, 0))
    return pl.pallas_call(
        _ffn_kernel,
        grid=(n // tm,),
        in_specs=[row, _const_spec(w1.shape), _const_spec(w2.shape), _const_spec((1, d)), _const_spec((1, d))],
        out_specs=row,
        out_shape=jax.ShapeDtypeStruct((n, d), F32),
        compiler_params=_params("parallel"),
        name="relu2_mlp",
    )(x1, w1, w2, g.reshape(1, d), b.reshape(1, d))


def _hybrid_layer(xn, entry_ln, tabs, B, L, p):
    w_cat, w_vt, w_gate = _expand_in_weight(p["w_in"])
    (u_pool, u_ssm, q2, ksx, kw, kc, vs_t, vw_t, vc, gn) = _in_projection(xn, entry_ln, w_cat, w_vt, tabs, B, L)
    mats = _s5_matrices(p["ssm_lam_re"], p["ssm_lam_im"], p["ssm_log_dt"], p["ssm_b_re"], p["ssm_b_im"],
                        p["ssm_c_re"], p["ssm_c_im"], p["ssm_d"])
    z_ssm = _s5_mixer(u_ssm, mats, B, L)
    kcmp, vcmp_t = _compress(kc, vc, _compress_weights(p["cmp_pe_k"], p["cmp_wk1"], p["cmp_wk2"], False),
                             _compress_weights(p["cmp_pe_v"], p["cmp_wv1"], p["cmp_wv2"], True), B, L)
    y_nsa = _nsa_attention(q2, gn, kcmp, vcmp_t, ksx, vs_t, kw, vw_t, B, L)
    cast = lambda a: a.astype(MXU_DTYPE)
    row = lambda a: a.reshape(1, -1).astype(F32)
    w = dict(w_gate=w_gate, w_pool=cast(p["w_pool"]), pool_scale=row(p["pool_scale"]), w_glu=cast(p["w_glu"]),
             b_glu=row(p["b_glu"]), w_up_pool=cast(p["w_up_pool"]), w_up_ssm=cast(p["w_up_ssm"]),
             w_up_nsa=cast(p["w_up_nsa"]), w_out=cast(p["w_out"]), ln1_g=row(p["ln1_g"]), ln1_b=row(p["ln1_b"]))
    x1 = _merge(xn, entry_ln, u_pool, z_ssm, y_nsa, w, L)
    return _ffn(x1, cast(p["w_ff1"]), cast(p["w_ff2"]), p["ln2_g"], p["ln2_b"])


_LAYER_PARAMS = ("w_in", "w_pool", "pool_scale", "ssm_lam_re", "ssm_lam_im", "ssm_log_dt", "ssm_b_re", "ssm_b_im",
                 "ssm_c_re", "ssm_c_im", "ssm_d", "w_glu", "b_glu", "cmp_pe_k", "cmp_pe_v", "cmp_wk1", "cmp_wk2",
                 "cmp_wv1", "cmp_wv2", "w_up_pool", "w_up_ssm", "w_up_nsa", "w_out", "ln1_g", "ln1_b", "w_ff1",
                 "w_ff2", "ln2_g", "ln2_b")


def kernel(x, ln_in_g, ln_in_b, w_in, w_pool, pool_scale, ssm_lam_re, ssm_lam_im, ssm_log_dt, ssm_b_re, ssm_b_im, ssm_c_re, ssm_c_im, ssm_d, w_glu, b_glu, cmp_pe_k, cmp_pe_v, cmp_wk1, cmp_wk2, cmp_wv1, cmp_wv2, w_up_pool, w_up_ssm, w_up_nsa, w_out, ln1_g, ln1_b, w_ff1, w_ff2, ln2_g, ln2_b):
    B, L, D = x.shape
    assert D == D_MODEL and L % ROW_TILE == 0
    stacked = dict(zip(_LAYER_PARAMS, (w_in, w_pool, pool_scale, ssm_lam_re, ssm_lam_im, ssm_log_dt, ssm_b_re,
                                       ssm_b_im, ssm_c_re, ssm_c_im, ssm_d, w_glu, b_glu, cmp_pe_k, cmp_pe_v,
                                       cmp_wk1, cmp_wk2, cmp_wv1, cmp_wv2, w_up_pool, w_up_ssm, w_up_nsa, w_out,
                                       ln1_g, ln1_b, w_ff1, w_ff2, ln2_g, ln2_b)))
    tabs = _rope_tables(L)
    xn = x.reshape(B * L, D)
    entry_ln = (ln_in_g.reshape(1, D), ln_in_b.reshape(1, D))
    for i in range(w_in.shape[0]):
        xn = _hybrid_layer(xn, entry_ln if i == 0 else (), tabs, B, L, {k: v[i] for k, v in stacked.items()})
    return xn.reshape(B, L, D)
```
